```python
import math
import jax, jax.numpy as jnp
from jax import lax
import numpy as np

D_MODEL = 2048
BATCH = 1
SEQ = 16384
DEPTH = 4
DEC_BATCH = 16
DEC_SEQ = 32
PAST_LEN = 1024

CHUNK = 64
N_MIXERS = 2
N_GDN = (DEPTH + 1) // 2
N_ATT = DEPTH // 2
EPS = 1e-6
GDN_QK_HEADS = 16
GDN_V_HEADS = 32
GDN_DK = 128
GDN_DV = 128
CONV_W = 4
GDN_QKV = 2 * GDN_QK_HEADS * GDN_DK + GDN_V_HEADS * GDN_DV
GDN_Z = GDN_V_HEADS * GDN_DV
GDN_PROJ = GDN_QKV + GDN_Z + 2 * GDN_V_HEADS
ATT_HEADS = 16
ATT_DH = D_MODEL // ATT_HEADS
LEFT_CHUNKS = 8
BAND_PAST = LEFT_CHUNKS * CHUNK
BAND = BAND_PAST + CHUNK
MAX_REL = 256
D_FF = -(-8 * D_MODEL // (3 * 256)) * 256

kernel_name = 'hybrid_gdn_chunkband_stream_step'


def rmsnorm(x, w):
    xf = x.astype(jnp.float32)
    y = xf * lax.rsqrt(jnp.mean(xf * xf, axis=-1, keepdims=True) + EPS)
    return (y * w.astype(jnp.float32)).astype(x.dtype)


def l2norm(x):
    xf = x.astype(jnp.float32)
    return xf * lax.rsqrt(jnp.sum(xf * xf, axis=-1, keepdims=True) + EPS)


def swiglu(h, w_gate, w_up, w_down):
    return (jax.nn.silu(h @ w_gate) * (h @ w_up)) @ w_down


def gated_delta_chunked(q, k, v, g, beta, s0, chunk):
    b, l, h, dk = q.shape
    dv = v.shape[-1]
    n = l // chunk

    def blocks(t):
        t = t.reshape((b, n, chunk) + t.shape[2:])
        return jnp.swapaxes(jnp.swapaxes(t, 0, 1), 2, 3)

    idx = jnp.arange(chunk)
    incl = idx[:, None] >= idx[None, :]
    strict = idx[:, None] > idx[None, :]
    eye = jnp.eye(chunk, dtype=jnp.float32)

    def step(s, blk):
        qc, kc, vc, gc, bc = blk
        gcum = jnp.cumsum(gc, axis=-1)
        decay = jnp.exp(jnp.where(incl, gcum[..., :, None] - gcum[..., None, :], -jnp.inf))
        kk = jnp.einsum('bhid,bhjd->bhij', kc, kc)
        a_mat = eye + jnp.where(strict, kk * decay * bc[..., :, None], 0.0)
        rhs = jnp.concatenate([vc * bc[..., None], kc * (bc * jnp.exp(gcum))[..., None]], axis=-1)
        sol = lax.linalg.triangular_solve(a_mat, rhs, left_side=True, lower=True, unit_diagonal=True)
        u, w = sol[..., :dv], sol[..., dv:]
        v_new = u - jnp.einsum('bhck,bhkv->bhcv', w, s)
        qk = jnp.einsum('bhid,bhjd->bhij', qc, kc) * decay
        o = (jnp.einsum('bhck,bhkv->bhcv', qc * jnp.exp(gcum)[..., None], s)
             + jnp.einsum('bhij,bhjv->bhiv', qk, v_new))
        g_last = gcum[..., -1:]
        s_new = (s * jnp.exp(g_last)[..., None]
                 + jnp.einsum('bhck,bhcv->bhkv', kc * jnp.exp(g_last - gcum)[..., None], v_new))
        return s_new, o

    s, o = lax.scan(step, s0, (blocks(q), blocks(k), blocks(v), blocks(g), blocks(beta)))
    o = jnp.swapaxes(jnp.swapaxes(o, 2, 3), 0, 1).reshape(b, l, h, dv)
    return o, s


def gdn_mixer(h, conv_prev, s0, chunk, w_in, w_conv, a_log, dt_bias, w_norm, w_out):
    b, l, _ = h.shape
    proj = h @ w_in
    qkv = proj[..., :GDN_QKV]
    z = proj[..., GDN_QKV:GDN_QKV + GDN_Z]
    beta_in = proj[..., GDN_QKV + GDN_Z:GDN_QKV + GDN_Z + GDN_V_HEADS]
    a_in = proj[..., GDN_QKV + GDN_Z + GDN_V_HEADS:]
    xc = jnp.concatenate([conv_prev.astype(qkv.dtype), qkv], axis=1)
    conv = xc[:, 0:l] * w_conv[0]
    for j in range(1, CONV_W):
        conv = conv + xc[:, j:j + l] * w_conv[j]
    conv = jax.nn.silu(conv)
    new_conv = xc[:, l:]
    nqk = GDN_QK_HEADS * GDN_DK
    rep = GDN_V_HEADS // GDN_QK_HEADS
    q = conv[..., :nqk].reshape(b, l, GDN_QK_HEADS, GDN_DK)
    k = conv[..., nqk:2 * nqk].reshape(b, l, GDN_QK_HEADS, GDN_DK)
    v = conv[..., 2 * nqk:].reshape(b, l, GDN_V_HEADS, GDN_DV).astype(jnp.float32)
    q = jnp.repeat(l2norm(q) * GDN_DK ** -0.5, rep, axis=2)
    k = jnp.repeat(l2norm(k), rep, axis=2)
    beta = jax.nn.sigmoid(beta_in.astype(jnp.float32))
    g = -jnp.exp(a_log.astype(jnp.float32)) * jax.nn.softplus(a_in.astype(jnp.float32) + dt_bias.astype(jnp.float32))
    o, s = gated_delta_chunked(q, k, v, g, beta, s0.astype(jnp.float32), chunk)
    o = o * lax.rsqrt(jnp.mean(o * o, axis=-1, keepdims=True) + EPS) * w_norm.astype(jnp.float32)
    o = o * jax.nn.silu(z.reshape(b, l, GDN_V_HEADS, GDN_DV).astype(jnp.float32))
    y = o.reshape(b, l, GDN_Z).astype(h.dtype) @ w_out
    return y, new_conv, s


def rel_bias(table, q_pos, k_pos):
    rel = jnp.clip(q_pos[:, None] - k_pos[None, :], -MAX_REL, MAX_REL) + MAX_REL
    return table.astype(jnp.float32)[:, rel]


def attend(q, k, v, bias):
    s = jnp.einsum('bqhd,bkhd->bhqk', q, k).astype(jnp.float32) * ATT_DH ** -0.5 + bias
    p = jax.nn.softmax(s, axis=-1).astype(v.dtype)
    return jnp.einsum('bhqk,bkhd->bqhd', p, v)


def att_project(h, w_qkv, b_qkv):
    b, l, _ = h.shape
    q, k, v = jnp.split(h @ w_qkv + b_qkv, 3, axis=-1)
    shape = (b, l, ATT_HEADS, ATT_DH)
    return q.reshape(shape), k.reshape(shape), v.reshape(shape)


def att_prompt(h, w_qkv, b_qkv, table, w_o, b_o):
    b, l, _ = h.shape
    q, k, v = att_project(h, w_qkv, b_qkv)
    nc = l // CHUNK
    pad = ((0, 0), (BAND_PAST, 0), (0, 0), (0, 0))
    kp, vp = jnp.pad(k, pad), jnp.pad(v, pad)
    qc = jnp.swapaxes(q.reshape(b, nc, CHUNK, ATT_HEADS, ATT_DH), 0, 1)
    offs = jnp.arange(BAND)
    bias = rel_bias(table, jnp.arange(CHUNK) + BAND_PAST, offs)

    def one_chunk(args):
        c, q_blk = args
        start = c * CHUNK
        k_blk = lax.dynamic_slice_in_dim(kp, start, BAND, axis=1)
        v_blk = lax.dynamic_slice_in_dim(vp, start, BAND, axis=1)
        valid = start + offs >= BAND_PAST
        return attend(q_blk, k_blk, v_blk, jnp.where(valid, bias, -jnp.inf))

    o = lax.map(one_chunk, (jnp.arange(nc), qc))
    o = jnp.swapaxes(o, 0, 1).reshape(b, l, D_MODEL)
    keep = min(BAND_PAST, l)
    return o @ w_o + b_o, k[:, l - keep:], v[:, l - keep:]


def att_sample(h, cache_k, cache_v, w_qkv, b_qkv, table, w_o, b_o):
    b, l, _ = h.shape
    r = cache_k.shape[1]
    q, k, v = att_project(h, w_qkv, b_qkv)
    k_all = jnp.concatenate([cache_k.astype(k.dtype), k], axis=1)
    v_all = jnp.concatenate([cache_v.astype(v.dtype), v], axis=1)
    bias = rel_bias(table, jnp.arange(l) + r, jnp.arange(r + l))
    o = attend(q, k_all, v_all, bias).reshape(b, l, D_MODEL)
    return o @ w_o + b_o, k, v


def setup_inputs(seed: int = 0) -> dict:
    key = jax.random.key(seed)
    ks = jax.random.split(key, 24)
    f32 = jnp.float32

    def nrm(k, shape, scale):
        return jax.random.normal(k, shape, f32) * scale

    rows = min(BAND_PAST, PAST_LEN)
    dt = jnp.exp(jax.random.uniform(ks[10], (N_GDN, GDN_V_HEADS), f32, math.log(1e-3), math.log(1e-1)))
    return {
        'x_prompt': nrm(ks[0], (BATCH, SEQ, D_MODEL), 1.0),
        'x_sample': nrm(ks[1], (DEC_BATCH, DEC_SEQ, D_MODEL), 1.0),
        'state_gdn_rec': nrm(ks[2], (N_GDN, DEC_BATCH, GDN_V_HEADS, GDN_DK, GDN_DV), 0.1),
        'state_gdn_conv': nrm(ks[3], (N_GDN, DEC_BATCH, CONV_W - 1, GDN_QKV), 1.0),
        'cache_att_k': nrm(ks[4], (N_ATT, DEC_BATCH, rows, ATT_HEADS, ATT_DH), 1.0),
        'cache_att_v': nrm(ks[5], (N_ATT, DEC_BATCH, rows, ATT_HEADS, ATT_DH), 1.0),
        'norm_mix': 1.0 + nrm(ks[6], (DEPTH, D_MODEL), 0.02),
        'norm_ffn': 1.0 + nrm(ks[7], (DEPTH, D_MODEL), 0.02),
        'norm_final': 1.0 + nrm(ks[8], (D_MODEL,), 0.02),
        'gdn_w_in': nrm(ks[9], (N_GDN, D_MODEL, GDN_PROJ), D_MODEL ** -0.5),
        'gdn_w_conv': nrm(ks[11], (N_GDN, CONV_W, GDN_QKV), 0.5),
        'gdn_a_log': jnp.log(jax.random.uniform(ks[12], (N_GDN, GDN_V_HEADS), f32, 1.0, 16.0)),
        'gdn_dt_bias': dt + jnp.log(-jnp.expm1(-dt)),
        'gdn_w_norm': 1.0 + nrm(ks[13], (N_GDN, GDN_DV), 0.02),
        'gdn_w_out': nrm(ks[14], (N_GDN, GDN_Z, D_MODEL), GDN_Z ** -0.5),
        'att_w_qkv': nrm(ks[15], (N_ATT, D_MODEL, 3 * D_MODEL), D_MODEL ** -0.5),
        'att_b_qkv': nrm(ks[16], (N_ATT, 3 * D_MODEL), 0.02),
        'att_rel_bias': nrm(ks[17], (N_ATT, ATT_HEADS, 2 * MAX_REL + 1), 0.5),
        'att_w_o': nrm(ks[18], (N_ATT, D_MODEL, D_MODEL), D_MODEL ** -0.5),
        'att_b_o': nrm(ks[19], (N_ATT, D_MODEL), 0.02),
        'ffn_w_gate': nrm(ks[20], (DEPTH, D_MODEL, D_FF), D_MODEL ** -0.5),
        'ffn_w_up': nrm(ks[21], (DEPTH, D_MODEL, D_FF), D_MODEL ** -0.5),
        'ffn_w_down': nrm(ks[22], (DEPTH, D_FF, D_MODEL), D_FF ** -0.5),
    }


def reference(x_prompt, x_sample, state_gdn_rec, state_gdn_conv, cache_att_k, cache_att_v,
              norm_mix, norm_ffn, norm_final,
              gdn_w_in, gdn_w_conv, gdn_a_log, gdn_dt_bias, gdn_w_norm, gdn_w_out,
              att_w_qkv, att_b_qkv, att_rel_bias, att_w_o, att_b_o,
              ffn_w_gate, ffn_w_up, ffn_w_down):
    xp, xs = x_prompt, x_sample
    bp = xp.shape[0]
    p_rec, p_conv, p_k, p_v = [], [], [], []
    s_rec, s_conv, s_k, s_v = [], [], [], []
    for layer in range(DEPTH):
        j = layer // N_MIXERS
        hp = rmsnorm(xp, norm_mix[layer])
        hs = rmsnorm(xs, norm_mix[layer])
        if layer % N_MIXERS == 0:
            w = (gdn_w_in[j], gdn_w_conv[j], gdn_a_log[j], gdn_dt_bias[j], gdn_w_norm[j], gdn_w_out[j])
            conv0 = jnp.zeros((bp, CONV_W - 1, GDN_QKV), hp.dtype)
            rec0 = jnp.zeros((bp, GDN_V_HEADS, GDN_DK, GDN_DV), jnp.float32)
            yp, cp, rp = gdn_mixer(hp, conv0, rec0, CHUNK, *w)
            ys, cs, rs = gdn_mixer(hs, state_gdn_conv[j], state_gdn_rec[j], xs.shape[1], *w)
            p_conv.append(cp)
            p_rec.append(rp.astype(x_prompt.dtype))
            s_conv.append(cs.astype(state_gdn_conv.dtype))
            s_rec.append(rs.astype(state_gdn_rec.dtype))
        else:
            w = (att_w_qkv[j], att_b_qkv[j], att_rel_bias[j], att_w_o[j], att_b_o[j])
            yp, kp, vp = att_prompt(hp, *w)
            ys, kn, vn = att_sample(hs, cache_att_k[j], cache_att_v[j], *w)
            p_k.append(kp)
            p_v.append(vp)
            s_k.append(kn)
            s_v.append(vn)
        xp = xp + yp
        xs = xs + ys
        xp = xp + swiglu(rmsnorm(xp, norm_ffn[layer]), ffn_w_gate[layer], ffn_w_up[layer], ffn_w_down[layer])
        xs = xs + swiglu(rmsnorm(xs, norm_ffn[layer]), ffn_w_gate[layer], ffn_w_up[layer], ffn_w_down[layer])
    y_prompt = rmsnorm(xp, norm_final)
    y_sample = rmsnorm(xs, norm_final)
    prompt_gdn_rec = jnp.stack(p_rec, axis=0)
    prompt_gdn_conv = jnp.stack(p_conv, axis=0)
    prompt_att_k = jnp.stack(p_k, axis=0)
    prompt_att_v = jnp.stack(p_v, axis=0)
    sample_gdn_rec = jnp.stack(s_rec, axis=0)
    sample_gdn_conv = jnp.stack(s_conv, axis=0)
    sample_att_k = jnp.stack(s_k, axis=0)
    sample_att_v = jnp.stack(s_v, axis=0)
    return (y_prompt, y_sample, prompt_gdn_rec, prompt_gdn_conv, prompt_att_k, prompt_att_v,
            sample_gdn_rec, sample_gdn_conv, sample_att_k, sample_att_v)
```

```python
import functools
import math

import jax
import jax.numpy as jnp
from jax import lax
from jax.experimental import pallas as pl
from jax.experimental.pallas import tpu as pltpu

EPS = 1e-6
CHUNK = 64
GDN_QK_HEADS = 16
GDN_V_HEADS = 32
GDN_DK = 128
GDN_DV = 128
CONV_W = 4
GDN_QKV = 2 * GDN_QK_HEADS * GDN_DK + GDN_V_HEADS * GDN_DV
GDN_Z = GDN_V_HEADS * GDN_DV
ATT_HEADS = 16
ATT_DH = 128
BAND_PAST = 512
MAX_REL = 256
LANES = 128
SUBLANES = 8
NEG_BIG = -1e30
VMEM_LIMIT_BYTES = 56 * 1024 * 1024

BF16 = jnp.bfloat16
F32 = jnp.float32
HIGHEST = lax.Precision.HIGHEST


def _cparams(*sem):
    return pltpu.CompilerParams(dimension_semantics=sem, vmem_limit_bytes=VMEM_LIMIT_BYTES)


def _dot(a, b):
    return jnp.dot(a, b, preferred_element_type=F32)


def _dot_nt(a, b):
    return lax.dot_general(a, b, (((1,), (1,)), ((), ())), preferred_element_type=F32)


def _dot_tn(a, b):
    return lax.dot_general(a, b, (((0,), (0,)), ((), ())), preferred_element_type=F32)


def _rms(x, w):
    return x * lax.rsqrt(jnp.mean(x * x, axis=-1, keepdims=True) + EPS) * w


def _sigmoid(x):
    return 1.0 / (1.0 + jnp.exp(-x))


def _row_tile(m, pref):
    t = min(m, pref)
    assert m % t == 0, (m, t)
    return t


def _norm_proj_kernel(*refs, has_bias, has_w2):
    x_ref, nw_ref, w_ref = refs[:3]
    pos = 3
    b_ref = w2_ref = None
    if has_bias:
        b_ref = refs[pos]
        pos += 1
    if has_w2:
        w2_ref = refs[pos]
        pos += 1
    o_ref = refs[pos]
    pos += 1
    o2_ref = None
    if has_w2:
        o2_ref = refs[pos]
        pos += 1
    h_ref = refs[pos]

    @pl.when(pl.program_id(1) == 0)
    def _():
        h = _rms(x_ref[...], nw_ref[...]).astype(BF16)
        h_ref[...] = h
        if has_w2:
            o2_ref[...] = _dot(h, w2_ref[...])

    acc = _dot(h_ref[...], w_ref[...])
    if has_bias:
        acc = acc + b_ref[...]
    o_ref[...] = acc


def norm_proj(x, nw, w, bias=None, w2=None, *, tm=512, tn=1024):
    m, d = x.shape
    n = w.shape[1]
    tm = _row_tile(m, tm)
    tn = _row_tile(n, tn)
    in_specs = [pl.BlockSpec((tm, d), lambda i, j: (i, 0)),
                pl.BlockSpec((1, d), lambda i, j: (0, 0)),
                pl.BlockSpec((d, tn), lambda i, j: (0, j))]
    args = [x, nw.reshape(1, d), w]
    if bias is not None:
        in_specs.append(pl.BlockSpec((1, tn), lambda i, j: (0, j)))
        args.append(bias.reshape(1, n))
    out_shape = [jax.ShapeDtypeStruct((m, n), F32)]
    out_specs = [pl.BlockSpec((tm, tn), lambda i, j: (i, j))]
    if w2 is not None:
        n2 = w2.shape[1]
        in_specs.append(pl.BlockSpec((d, n2), lambda i, j: (0, 0)))
        args.append(w2)
        out_shape.append(jax.ShapeDtypeStruct((m, n2), F32))
        out_specs.append(pl.BlockSpec((tm, n2), lambda i, j: (i, 0)))
    res = pl.pallas_call(
        functools.partial(_norm_proj_kernel, has_bias=bias is not None, has_w2=w2 is not None),
        grid=(m // tm, n // tn),
        in_specs=in_specs, out_specs=out_specs, out_shape=out_shape,
        scratch_shapes=[pltpu.VMEM((tm, d), BF16)],
        compiler_params=_cparams("parallel", "arbitrary"),
        name="norm_proj",
    )(*args)
    return res if w2 is not None else res[0]


def _ffn_kernel(x_ref, nw_ref, wg_ref, wu_ref, wd_ref, o_ref, h_ref):
    @pl.when(pl.program_id(1) == 0)
    def _():
        x = x_ref[...]
        h_ref[...] = _rms(x, nw_ref[...]).astype(BF16)
        o_ref[...] = x

    h = h_ref[...]
    g = _dot(h, wg_ref[...])
    u = _dot(h, wu_ref[...])
    a = (g * _sigmoid(g) * u).astype(BF16)
    o_ref[...] += _dot(a, wd_ref[...])


def ffn(x, nw, wg, wu, wd, *, tm=512, tf=512):
    m, d = x.shape
    f = wg.shape[1]
    tm = _row_tile(m, tm)
    tf = _row_tile(f, tf)
    return pl.pallas_call(
        _ffn_kernel,
        grid=(m // tm, f // tf),
        in_specs=[pl.BlockSpec((tm, d), lambda i, j: (i, 0)),
                  pl.BlockSpec((1, d), lambda i, j: (0, 0)),
                  pl.BlockSpec((d, tf), lambda i, j: (0, j)),
                  pl.BlockSpec((d, tf), lambda i, j: (0, j)),
                  pl.BlockSpec((tf, d), lambda i, j: (j, 0))],
        out_specs=pl.BlockSpec((tm, d), lambda i, j: (i, 0)),
        out_shape=jax.ShapeDtypeStruct((m, d), F32),
        scratch_shapes=[pltpu.VMEM((tm, d), BF16)],
        compiler_params=_cparams("parallel", "arbitrary"),
        name="ffn",
    )(x, nw.reshape(1, d), wg, wu, wd)


def _out_proj_kernel(*refs, gated, has_bias):
    a_ref = refs[0]
    pos = 1
    z_ref = b_ref = None
    if gated:
        z_ref = refs[pos]
        pos += 1
    w_ref = refs[pos]
    pos += 1
    if has_bias:
        b_ref = refs[pos]
        pos += 1
    res_ref, o_ref, lhs_ref = refs[pos:pos + 3]

    @pl.when(pl.program_id(1) == 0)
    def _():
        a = a_ref[...]
        if gated:
            z = z_ref[...]
            a = a * (z * _sigmoid(z))
        lhs_ref[...] = a.astype(BF16)

    acc = res_ref[...] + _dot(lhs_ref[...], w_ref[...])
    if has_bias:
        acc = acc + b_ref[...]
    o_ref[...] = acc


def out_proj(res, a, w, bias=None, z_src=None, z_col_block=0, *, tm=512, tn=1024):
    m, k = a.shape
    n = w.shape[1]
    tm = _row_tile(m, tm)
    tn = _row_tile(n, tn)
    in_specs = [pl.BlockSpec((tm, k), lambda i, j: (i, 0))]
    args = [a]
    if z_src is not None:
        in_specs.append(pl.BlockSpec((tm, k), lambda i, j: (i, z_col_block)))
        args.append(z_src)
    in_specs.append(pl.BlockSpec((k, tn), lambda i, j: (0, j)))
    args.append(w)
    if bias is not None:
        in_specs.append(pl.BlockSpec((1, tn), lambda i, j: (0, j)))
        args.append(bias.reshape(1, n))
    in_specs.append(pl.BlockSpec((tm, tn), lambda i, j: (i, j)))
    args.append(res)
    return pl.pallas_call(
        functools.partial(_out_proj_kernel, gated=z_src is not None, has_bias=bias is not None),
        grid=(m // tm, n // tn),
        in_specs=in_specs,
        out_specs=pl.BlockSpec((tm, tn), lambda i, j: (i, j)),
        out_shape=jax.ShapeDtypeStruct((m, n), F32),
        scratch_shapes=[pltpu.VMEM((tm, k), BF16)],
        compiler_params=_cparams("parallel", "arbitrary"),
        name="out_proj",
    )(*args)


def _rmsnorm_kernel(x_ref, w_ref, o_ref):
    o_ref[...] = _rms(x_ref[...], w_ref[...])


def rmsnorm(x, w, *, tm=512):
    m, d = x.shape
    tm = _row_tile(m, tm)
    return pl.pallas_call(
        _rmsnorm_kernel,
        grid=(m // tm,),
        in_specs=[pl.BlockSpec((tm, d), lambda i: (i, 0)), pl.BlockSpec((1, d), lambda i: (0, 0))],
        out_specs=pl.BlockSpec((tm, d), lambda i: (i, 0)),
        out_shape=jax.ShapeDtypeStruct((m, d), F32),
        compiler_params=_cparams("parallel"),
        name="final_rmsnorm",
    )(x, w.reshape(1, d))


def _gdn_prep_kernel(x_ref, p_ref, st_ref, w_ref, o_ref, *, tiles_per_seq, n_q_blocks, n_qk_blocks):
    i = pl.program_id(0)
    j = pl.program_id(1)
    tr, tc = x_ref.shape
    x = x_ref[...]
    w = w_ref[...]
    first = (i % tiles_per_seq) == 0
    prev8 = jnp.where(first, st_ref[0], p_ref[...])

    def conv_of(rows):
        acc = rows * w[CONV_W - 1:CONV_W]
        for s in range(1, CONV_W):
            acc = acc + pltpu.roll(rows, s, axis=0) * w[CONV_W - 1 - s:CONV_W - s]
        return acc

    def silu(c):
        return c * _sigmoid(c)

    act_main = silu(conv_of(x))
    act_head = silu(conv_of(jnp.concatenate([prev8, x[:SUBLANES]], axis=0))[SUBLANES:])

    @pl.when(j < n_qk_blocks)
    def _():
        scale = jnp.where(j < n_q_blocks, GDN_DK ** -0.5, 1.0)
        for act, nrows in ((act_main, tr), (act_head, SUBLANES)):
            for h in range(tc // LANES):
                a = act[:, h * LANES:(h + 1) * LANES]
                o_ref[:nrows, h * LANES:(h + 1) * LANES] = a * (
                    lax.rsqrt(jnp.sum(a * a, axis=-1, keepdims=True) + EPS) * scale)

    @pl.when(j >= n_qk_blocks)
    def _():
        o_ref[...] = act_main
        o_ref[:SUBLANES, :] = act_head


def gdn_prep(qkvz, conv_state8, w_conv, seq_len, *, tr, tc=1024):
    m = qkvz.shape[0]
    assert seq_len % tr == 0 and tr % SUBLANES == 0
    tiles_per_seq = seq_len // tr
    nq = GDN_QK_HEADS * GDN_DK // tc
    return pl.pallas_call(
        functools.partial(_gdn_prep_kernel, tiles_per_seq=tiles_per_seq, n_q_blocks=nq, n_qk_blocks=2 * nq),
        grid=(m // tr, GDN_QKV // tc),
        in_specs=[pl.BlockSpec((tr, tc), lambda i, j: (i, j)),
                  pl.BlockSpec((SUBLANES, tc), lambda i, j: (jnp.maximum(i * (tr // SUBLANES) - 1, 0), j)),
                  pl.BlockSpec((1, SUBLANES, tc), lambda i, j: (i // tiles_per_seq, 0, j)),
                  pl.BlockSpec((CONV_W, tc), lambda i, j: (0, j))],
        out_specs=pl.BlockSpec((tr, tc), lambda i, j: (i, j)),
        out_shape=jax.ShapeDtypeStruct((m, GDN_QKV), F32),
        compiler_params=_cparams("parallel", "parallel"),
        name="gdn_prep",
    )(qkvz, qkvz, conv_state8, w_conv)


def _gdn_core_kernel(q_ref, k_ref, v_ref, ba_ref, gp_ref, wn_ref, s0_ref, o_ref, s_ref, *, nh):
    hg = pl.program_id(1)
    c = pl.program_id(2)
    C = q_ref.shape[0]

    @pl.when(c == 0)
    def _():
        s_ref[...] = s0_ref[...]

    ba = ba_ref[...]
    gp = gp_ref[...]
    beta_t = _sigmoid(ba)
    xs = ba + gp[1:2]
    softplus = jnp.maximum(xs, 0.0) + jnp.log1p(jnp.exp(-jnp.abs(xs)))
    g_t = -jnp.exp(gp[0:1]) * softplus
    row = lax.broadcasted_iota(jnp.int32, (C, C), 0)
    col = lax.broadcasted_iota(jnp.int32, (C, C), 1)
    incl = row >= col
    strict = row > col
    tri = jnp.where(incl, 1.0, 0.0).astype(F32)
    gcum_t = jnp.dot(tri, g_t, precision=HIGHEST, preferred_element_type=F32)
    gcum_r = gcum_t.T
    lane = lax.broadcasted_iota(jnp.int32, (C, LANES), 1)
    sub = lax.broadcasted_iota(jnp.int32, (LANES, C), 0)
    n_levels = int(math.log2(C))
    assert 1 << n_levels == C

    for r in range(nh):
        hv = hg * nh + r
        gc_col = jnp.sum(jnp.where(lane == GDN_V_HEADS + hv, gcum_t, 0.0), axis=1, keepdims=True)
        b_col = jnp.sum(jnp.where(lane == hv, beta_t, 0.0), axis=1, keepdims=True)
        gc_row = jnp.sum(jnp.where(sub == GDN_V_HEADS + hv, gcum_r, 0.0), axis=0, keepdims=True)
        g_last = gc_row[:, C - 1:C]
        qh = (r * GDN_QK_HEADS) // GDN_V_HEADS
        q = q_ref[:, qh * GDN_DK:(qh + 1) * GDN_DK]
        k = k_ref[:, qh * GDN_DK:(qh + 1) * GDN_DK]
        v = v_ref[:, r * GDN_DV:(r + 1) * GDN_DV]
        s = s_ref[0, r]
        e_col = jnp.exp(gc_col)
        decay = jnp.exp(jnp.where(incl, gc_col - gc_row, -jnp.inf))
        kb = k.astype(BF16)
        qb = q.astype(BF16)
        kk = _dot_nt(kb, kb)
        p = -jnp.where(strict, kk * decay * b_col, 0.0)
        y = jnp.concatenate([v * b_col, k * (b_col * e_col)], axis=1)
        for lvl in range(n_levels):
            y = y + jnp.dot(p, y, precision=HIGHEST, preferred_element_type=F32)
            if lvl + 1 < n_levels:
                p = jnp.dot(p, p, precision=HIGHEST, preferred_element_type=F32)
        u = y[:, :GDN_DV]
        w = y[:, GDN_DV:]
        sb = s.astype(BF16)
        v_new = u - _dot(w.astype(BF16), sb)
        vb = v_new.astype(BF16)
        qk = _dot_nt(qb, kb) * decay
        o = _dot((q * e_col).astype(BF16), sb) + _dot(qk.astype(BF16), vb)
        s_ref[0, r] = s * jnp.exp(g_last) + _dot_tn((k * jnp.exp(g_last - gc_col)).astype(BF16), vb)
        o = o * lax.rsqrt(jnp.mean(o * o, axis=-1, keepdims=True) + EPS) * wn_ref[...]
        o_ref[:, r * GDN_DV:(r + 1) * GDN_DV] = o


def gdn_core(conv, ba, gate_params, w_norm, s0, seq_len, chunk, *, nh=4):
    m = conv.shape[0]
    b = m // seq_len
    nc = seq_len // chunk
    assert b * seq_len == m and nc * chunk == seq_len and GDN_V_HEADS % nh == 0 and nh % 2 == 0
    nqk = nh * GDN_QK_HEADS // GDN_V_HEADS
    ngroups = GDN_V_HEADS // nh
    k_off = GDN_QK_HEADS // nqk
    v_off = 2 * GDN_QK_HEADS * GDN_DK // (nh * GDN_DV)
    row = lambda bi, hg, c: bi * nc + c
    return pl.pallas_call(
        functools.partial(_gdn_core_kernel, nh=nh),
        grid=(b, ngroups, nc),
        in_specs=[pl.BlockSpec((chunk, nqk * GDN_DK), lambda bi, hg, c: (row(bi, hg, c), hg)),
                  pl.BlockSpec((chunk, nqk * GDN_DK), lambda bi, hg, c: (row(bi, hg, c), k_off + hg)),
                  pl.BlockSpec((chunk, nh * GDN_DV), lambda bi, hg, c: (row(bi, hg, c), v_off + hg)),
                  pl.BlockSpec((chunk, LANES), lambda bi, hg, c: (row(bi, hg, c), 0)),
                  pl.BlockSpec((2, LANES), lambda bi, hg, c: (0, 0)),
                  pl.BlockSpec((1, GDN_DV), lambda bi, hg, c: (0, 0)),
                  pl.BlockSpec((1, nh, GDN_DK, GDN_DV), lambda bi, hg, c: (bi, hg, 0, 0))],
        out_specs=[pl.BlockSpec((chunk, nh * GDN_DV), lambda bi, hg, c: (row(bi, hg, c), hg)),
                   pl.BlockSpec((1, nh, GDN_DK, GDN_DV), lambda bi, hg, c: (bi, hg, 0, 0))],
        out_shape=[jax.ShapeDtypeStruct((m, GDN_Z), F32),
                   jax.ShapeDtypeStruct((b, GDN_V_HEADS, GDN_DK, GDN_DV), F32)],
        compiler_params=_cparams("parallel", "parallel", "arbitrary"),
        name="gdn_core",
    )(conv, conv, conv, ba, gate_params, w_norm.reshape(1, GDN_DV), s0)


def _att_prompt_kernel(q_ref, kp_ref, kc_ref, vp_ref, vc_ref, bias_ref, o_ref):
    i = pl.program_id(1)
    qb_rows = q_ref.shape[0]
    q = q_ref[...].astype(BF16)
    k = jnp.concatenate([kp_ref[...], kc_ref[...]], axis=0).astype(BF16)
    v = jnp.concatenate([vp_ref[...], vc_ref[...]], axis=0).astype(BF16)
    s = _dot_nt(q, k) * ATT_DH ** -0.5 + bias_ref[0]
    col = lax.broadcasted_iota(jnp.int32, s.shape, 1)
    s = jnp.where(jnp.logical_and(i == 0, col < qb_rows), NEG_BIG, s)
    m = jnp.max(s, axis=-1, keepdims=True)
    p = jnp.exp(s - m)
    l = jnp.sum(p, axis=-1, keepdims=True)
    o_ref[...] = _dot(p.astype(BF16), v) / l


def att_prompt(qkv, bias_full):
    l = qkv.shape[0]
    qb = BAND_PAST
    assert l % qb == 0
    h = ATT_HEADS
    prev = lambda i: jnp.maximum(i - 1, 0)
    return pl.pallas_call(
        _att_prompt_kernel,
        grid=(h, l // qb),
        in_specs=[pl.BlockSpec((qb, ATT_DH), lambda hh, i: (i, hh)),
                  pl.BlockSpec((qb, ATT_DH), lambda hh, i: (prev(i), h + hh)),
                  pl.BlockSpec((qb, ATT_DH), lambda hh, i: (i, h + hh)),
                  pl.BlockSpec((qb, ATT_DH), lambda hh, i: (prev(i), 2 * h + hh)),
                  pl.BlockSpec((qb, ATT_DH), lambda hh, i: (i, 2 * h + hh)),
                  pl.BlockSpec((1, qb, 2 * qb), lambda hh, i: (hh, 0, 0))],
        out_specs=pl.BlockSpec((qb, ATT_DH), lambda hh, i: (i, hh)),
        out_shape=jax.ShapeDtypeStruct((l, h * ATT_DH), F32),
        compiler_params=_cparams("parallel", "parallel"),
        name="att_prompt",
    )(qkv, qkv, qkv, qkv, qkv, bias_full)


def _att_sample_kernel(qkv_ref, ck_ref, cv_ref, bc_ref, bn_ref, o_ref):
    d = ATT_HEADS * ATT_DH
    for h in range(ATT_HEADS):
        sl = slice(h * ATT_DH, (h + 1) * ATT_DH)
        q = qkv_ref[:, sl].astype(BF16)
        kn = qkv_ref[:, d + h * ATT_DH:d + (h + 1) * ATT_DH].astype(BF16)
        vn = qkv_ref[:, 2 * d + h * ATT_DH:2 * d + (h + 1) * ATT_DH].astype(BF16)
        ck = ck_ref[0, :, sl].astype(BF16)
        cv = cv_ref[0, :, sl].astype(BF16)
        s1 = _dot_nt(q, ck) * ATT_DH ** -0.5 + bc_ref[h]
        s2 = _dot_nt(q, kn) * ATT_DH ** -0.5 + bn_ref[h]
        m = jnp.maximum(jnp.max(s1, axis=-1, keepdims=True), jnp.max(s2, axis=-1, keepdims=True))
        p1 = jnp.exp(s1 - m)
        p2 = jnp.exp(s2 - m)
        l = jnp.sum(p1, axis=-1, keepdims=True) + jnp.sum(p2, axis=-1, keepdims=True)
        o_ref[:, sl] = (_dot(p1.astype(BF16), cv) + _dot(p2.astype(BF16), vn)) / l


def att_sample(qkv, cache_k, cache_v, bias_cache, bias_new, seq_len):
    m = qkv.shape[0]
    b = m // seq_len
    r = cache_k.shape[1]
    d = ATT_HEADS * ATT_DH
    return pl.pallas_call(
        _att_sample_kernel,
        grid=(b,),
        in_specs=[pl.BlockSpec((seq_len, 3 * d), lambda i: (i, 0)),
                  pl.BlockSpec((1, r, d), lambda i: (i, 0, 0)),
                  pl.BlockSpec((1, r, d), lambda i: (i, 0, 0)),
                  pl.BlockSpec((ATT_HEADS, seq_len, r), lambda i: (0, 0, 0)),
                  pl.BlockSpec((ATT_HEADS, seq_len, seq_len), lambda i: (0, 0, 0))],
        out_specs=pl.BlockSpec((seq_len, d), lambda i: (i, 0)),
        out_shape=jax.ShapeDtypeStruct((m, d), F32),
        compiler_params=_cparams("parallel"),
        name="att_sample",
    )(qkv, cache_k, cache_v, bias_cache, bias_new)


def _rel_bias(table, q_pos, k_pos):
    rel = jnp.clip(q_pos[:, None] - k_pos[None, :], -MAX_REL, MAX_REL) + MAX_REL
    return table[:, rel]


def _prompt_bias(table):
    qb = BAND_PAST
    r = jnp.arange(qb)
    w = jnp.arange(2 * qb)
    bias = _rel_bias(table, r + qb, w)
    start = (r // CHUNK) * CHUNK
    in_band = jnp.logical_and(w[None, :] >= start[:, None], w[None, :] < start[:, None] + BAND_PAST + CHUNK)
    return jnp.where(in_band[None], bias, NEG_BIG)


def _gdn_layer(xp, xs, nw, w_in, w_conv, a_log, dt_bias, w_norm, w_out, state_rec, state_conv, dec_seq):
    w_main = w_in[:, :GDN_QKV + GDN_Z].astype(BF16)
    w_ba = jnp.pad(w_in[:, GDN_QKV + GDN_Z:], ((0, 0), (0, LANES - 2 * GDN_V_HEADS))).astype(BF16)
    w_out_b = w_out.astype(BF16)
    gate_params = jnp.zeros((2, LANES), F32)
    gate_params = gate_params.at[0, GDN_V_HEADS:2 * GDN_V_HEADS].set(a_log)
    gate_params = gate_params.at[1, GDN_V_HEADS:2 * GDN_V_HEADS].set(dt_bias)
    z_blk = GDN_QKV // GDN_Z
    outs = []
    for x, seq_len, chunk, conv0, rec0, tr in (
            (xp, xp.shape[0], CHUNK, jnp.zeros((1, CONV_W - 1, GDN_QKV), F32),
             jnp.zeros((1, GDN_V_HEADS, GDN_DK, GDN_DV), F32), 256),
            (xs, dec_seq, dec_seq, state_conv, state_rec, dec_seq)):
        qkvz, ba = norm_proj(x, nw, w_main, w2=w_ba)
        conv8 = jnp.pad(conv0, ((0, 0), (SUBLANES - (CONV_W - 1), 0), (0, 0)))
        conv = gdn_prep(qkvz, conv8, w_conv, seq_len, tr=tr)
        o, rec = gdn_core(conv, ba, gate_params, w_norm, rec0, seq_len, chunk)
        x_new = out_proj(x, o, w_out_b, z_src=qkvz, z_col_block=z_blk, tm=256)
        nb = x.shape[0] // seq_len
        new_conv = qkvz[:, :GDN_QKV].reshape(nb, seq_len, GDN_QKV)[:, seq_len - (CONV_W - 1):]
        outs.append((x_new, new_conv, rec))
    return outs


def _att_layer(xp, xs, nw, w_qkv, b_qkv, table, w_o, b_o, cache_k, cache_v, dec_seq):
    d = xp.shape[1]
    w_qkv_b = w_qkv.astype(BF16)
    w_o_b = w_o.astype(BF16)
    qkv_p = norm_proj(xp, nw, w_qkv_b, bias=b_qkv)
    o_p = att_prompt(qkv_p, _prompt_bias(table))
    xp_new = out_proj(xp, o_p, w_o_b, bias=b_o)
    lp = xp.shape[0]
    keep = min(BAND_PAST, lp)
    kp = qkv_p[lp - keep:, d:2 * d].reshape(1, keep, ATT_HEADS, ATT_DH)
    vp = qkv_p[lp - keep:, 2 * d:].reshape(1, keep, ATT_HEADS, ATT_DH)
    nb, r = cache_k.shape[0], cache_k.shape[1]
    qkv_s = norm_proj(xs, nw, w_qkv_b, bias=b_qkv)
    bias = _rel_bias(table, jnp.arange(dec_seq) + r, jnp.arange(r + dec_seq))
    o_s = att_sample(qkv_s, cache_k.reshape(nb, r, d), cache_v.reshape(nb, r, d), bias[:, :, :r], bias[:, :, r:], dec_seq)
    xs_new = out_proj(xs, o_s, w_o_b, bias=b_o)
    ks = qkv_s[:, d:2 * d].reshape(nb, dec_seq, ATT_HEADS, ATT_DH)
    vs = qkv_s[:, 2 * d:].reshape(nb, dec_seq, ATT_HEADS, ATT_DH)
    return (xp_new, kp, vp), (xs_new, ks, vs)


def kernel(x_prompt, x_sample, state_gdn_rec, state_gdn_conv, cache_att_k, cache_att_v, norm_mix, norm_ffn, norm_final, gdn_w_in, gdn_w_conv, gdn_a_log, gdn_dt_bias, gdn_w_norm, gdn_w_out, att_w_qkv, att_b_qkv, att_rel_bias, att_w_o, att_b_o, ffn_w_gate, ffn_w_up, ffn_w_down):
    bp, lp, d = x_prompt.shape
    bs, ls, _ = x_sample.shape
    assert bp == 1
    depth = norm_mix.shape[0]
    xp = x_prompt.reshape(bp * lp, d)
    xs = x_sample.reshape(bs * ls, d)
    p_rec, p_conv, p_k, p_v = [], [], [], []
    s_rec, s_conv, s_k, s_v = [], [], [], []
    for layer in range(depth):
        j = layer // 2
        if layer % 2 == 0:
            (xp, cp, rp), (xs, cs, rs) = _gdn_layer(
                xp, xs, norm_mix[layer], gdn_w_in[j], gdn_w_conv[j], gdn_a_log[j], gdn_dt_bias[j], gdn_w_norm[j],
                gdn_w_out[j], state_gdn_rec[j], state_gdn_conv[j], ls)
            p_conv.append(cp)
            p_rec.append(rp)
            s_conv.append(cs)
            s_rec.append(rs)
        else:
            (xp, kp, vp), (xs, kn, vn) = _att_layer(
                xp, xs, norm_mix[layer], att_w_qkv[j], att_b_qkv[j], att_rel_bias[j], att_w_o[j], att_b_o[j],
                cache_att_k[j], cache_att_v[j], ls)
            p_k.append(kp)
            p_v.append(vp)
            s_k.append(kn)
            s_v.append(vn)
        wg = ffn_w_gate[layer].astype(BF16)
        wu = ffn_w_up[layer].astype(BF16)
        wd = ffn_w_down[layer].astype(BF16)
        xp = ffn(xp, norm_ffn[layer], wg, wu, wd)
        xs = ffn(xs, norm_ffn[layer], wg, wu, wd)
    y_prompt = rmsnorm(xp, norm_final).reshape(bp, lp, d)
    y_sample = rmsnorm(xs, norm_final).reshape(bs, ls, d)
    return (y_prompt, y_sample, jnp.stack(p_rec), jnp.stack(p_conv), jnp.stack(p_k), jnp.stack(p_v),
            jnp.stack(s_rec), jnp.stack(s_conv), jnp.stack(s_k), jnp.stack(s_v))
```

```python
import functools
import math

import jax
import jax.numpy as jnp
from jax import lax
from jax.experimental import pallas as pl
from jax.experimental.pallas import tpu as pltpu

EPS = 1e-6
CHUNK = 64
GDN_QK_HEADS = 16
GDN_V_HEADS = 32
GDN_DK = 128
GDN_DV = 128
CONV_W = 4
GDN_QKV = 2 * GDN_QK_HEADS * GDN_DK + GDN_V_HEADS * GDN_DV
GDN_Z = GDN_V_HEADS * GDN_DV
ATT_HEADS = 16
ATT_DH = 128
BAND_PAST = 512
BAND = BAND_PAST + CHUNK
MAX_REL = 256
LANES = 128
SUBLANES = 8
NEG_BIG = -1e30
VMEM_LIMIT_BYTES = 56 * 1024 * 1024

BF16 = jnp.bfloat16
F32 = jnp.float32
HIGHEST = lax.Precision.HIGHEST


def _cparams(*sem):
    return pltpu.CompilerParams(dimension_semantics=sem, vmem_limit_bytes=VMEM_LIMIT_BYTES)


def _dot(a, b):
    return jnp.dot(a, b, preferred_element_type=F32)


def _dot_nt(a, b):
    return lax.dot_general(a, b, (((1,), (1,)), ((), ())), preferred_element_type=F32)


def _rms(x, w):
    return x * lax.rsqrt(jnp.mean(x * x, axis=-1, keepdims=True) + EPS) * w


def _sigmoid(x):
    return 1.0 / (1.0 + jnp.exp(-x))


def _row_tile(m, pref):
    t = min(m, pref)
    assert m % t == 0, (m, t)
    return t


def _norm_proj_kernel(*refs, has_bias, has_w2):
    x_ref, nw_ref, w_ref = refs[:3]
    pos = 3
    b_ref = w2_ref = None
    if has_bias:
        b_ref = refs[pos]
        pos += 1
    if has_w2:
        w2_ref = refs[pos]
        pos += 1
    o_ref = refs[pos]
    pos += 1
    o2_ref = None
    if has_w2:
        o2_ref = refs[pos]
        pos += 1
    h_ref = refs[pos]

    @pl.when(pl.program_id(1) == 0)
    def _():
        h = _rms(x_ref[...], nw_ref[...]).astype(BF16)
        h_ref[...] = h
        if has_w2:
            o2_ref[...] = _dot(h, w2_ref[...])

    acc = _dot(h_ref[...], w_ref[...])
    if has_bias:
        acc = acc + b_ref[...]
    o_ref[...] = acc


def norm_proj(x, nw, w, bias=None, w2=None, *, tm=512, tn=1024):
    m, d = x.shape
    n = w.shape[1]
    tm = _row_tile(m, tm)
    tn = _row_tile(n, tn)
    in_specs = [pl.BlockSpec((tm, d), lambda i, j: (i, 0)),
                pl.BlockSpec((1, d), lambda i, j: (0, 0)),
                pl.BlockSpec((d, tn), lambda i, j: (0, j))]
    args = [x, nw.reshape(1, d), w]
    if bias is not None:
        in_specs.append(pl.BlockSpec((1, tn), lambda i, j: (0, j)))
        args.append(bias.reshape(1, n))
    out_shape = [jax.ShapeDtypeStruct((m, n), F32)]
    out_specs = [pl.BlockSpec((tm, tn), lambda i, j: (i, j))]
    if w2 is not None:
        n2 = w2.shape[1]
        in_specs.append(pl.BlockSpec((d, n2), lambda i, j: (0, 0)))
        args.append(w2)
        out_shape.append(jax.ShapeDtypeStruct((m, n2), F32))
        out_specs.append(pl.BlockSpec((tm, n2), lambda i, j: (i, 0)))
    res = pl.pallas_call(
        functools.partial(_norm_proj_kernel, has_bias=bias is not None, has_w2=w2 is not None),
        grid=(m // tm, n // tn),
        in_specs=in_specs, out_specs=out_specs, out_shape=out_shape,
        scratch_shapes=[pltpu.VMEM((tm, d), BF16)],
        compiler_params=_cparams("parallel", "arbitrary"),
        name="norm_proj",
    )(*args)
    return res if w2 is not None else res[0]


def _ffn_kernel(x_ref, nw_ref, wg_ref, wu_ref, wd_ref, o_ref, h_ref):
    @pl.when(pl.program_id(1) == 0)
    def _():
        x = x_ref[...]
        h_ref[...] = _rms(x, nw_ref[...]).astype(BF16)
        o_ref[...] = x

    h = h_ref[...]
    g = _dot(h, wg_ref[...])
    u = _dot(h, wu_ref[...])
    a = (g * _sigmoid(g) * u).astype(BF16)
    o_ref[...] += _dot(a, wd_ref[...])


def ffn(x, nw, wg, wu, wd, *, tm=512, tf=512):
    m, d = x.shape
    f = wg.shape[1]
    tm = _row_tile(m, tm)
    tf = _row_tile(f, tf)
    return pl.pallas_call(
        _ffn_kernel,
        grid=(m // tm, f // tf),
        in_specs=[pl.BlockSpec((tm, d), lambda i, j: (i, 0)),
                  pl.BlockSpec((1, d), lambda i, j: (0, 0)),
                  pl.BlockSpec((d, tf), lambda i, j: (0, j)),
                  pl.BlockSpec((d, tf), lambda i, j: (0, j)),
                  pl.BlockSpec((tf, d), lambda i, j: (j, 0))],
        out_specs=pl.BlockSpec((tm, d), lambda i, j: (i, 0)),
        out_shape=jax.ShapeDtypeStruct((m, d), F32),
        scratch_shapes=[pltpu.VMEM((tm, d), BF16)],
        compiler_params=_cparams("parallel", "arbitrary"),
        name="ffn",
    )(x, nw.reshape(1, d), wg, wu, wd)


def _out_proj_kernel(*refs, gated, has_bias):
    a_ref = refs[0]
    pos = 1
    z_ref = b_ref = None
    if gated:
        z_ref = refs[pos]
        pos += 1
    w_ref = refs[pos]
    pos += 1
    if has_bias:
        b_ref = refs[pos]
        pos += 1
    res_ref, o_ref, lhs_ref = refs[pos:pos + 3]

    @pl.when(pl.program_id(1) == 0)
    def _():
        a = a_ref[...]
        if gated:
            z = z_ref[...]
            a = a * (z * _sigmoid(z))
        lhs_ref[...] = a.astype(BF16)

    acc = res_ref[...] + _dot(lhs_ref[...], w_ref[...])
    if has_bias:
        acc = acc + b_ref[...]
    o_ref[...] = acc


def out_proj(res, a, w, bias=None, z_src=None, z_col_block=0, *, tm=512, tn=1024):
    m, k = a.shape
    n = w.shape[1]
    tm = _row_tile(m, tm)
    tn = _row_tile(n, tn)
    in_specs = [pl.BlockSpec((tm, k), lambda i, j: (i, 0))]
    args = [a]
    if z_src is not None:
        in_specs.append(pl.BlockSpec((tm, k), lambda i, j: (i, z_col_block)))
        args.append(z_src)
    in_specs.append(pl.BlockSpec((k, tn), lambda i, j: (0, j)))
    args.append(w)
    if bias is not None:
        in_specs.append(pl.BlockSpec((1, tn), lambda i, j: (0, j)))
        args.append(bias.reshape(1, n))
    in_specs.append(pl.BlockSpec((tm, tn), lambda i, j: (i, j)))
    args.append(res)
    return pl.pallas_call(
        functools.partial(_out_proj_kernel, gated=z_src is not None, has_bias=bias is not None),
        grid=(m // tm, n // tn),
        in_specs=in_specs,
        out_specs=pl.BlockSpec((tm, tn), lambda i, j: (i, j)),
        out_shape=jax.ShapeDtypeStruct((m, n), F32),
        scratch_shapes=[pltpu.VMEM((tm, k), BF16)],
        compiler_params=_cparams("parallel", "arbitrary"),
        name="out_proj",
    )(*args)


def _rmsnorm_kernel(x_ref, w_ref, o_ref):
    o_ref[...] = _rms(x_ref[...], w_ref[...])


def rmsnorm(x, w, *, tm=512):
    m, d = x.shape
    tm = _row_tile(m, tm)
    return pl.pallas_call(
        _rmsnorm_kernel,
        grid=(m // tm,),
        in_specs=[pl.BlockSpec((tm, d), lambda i: (i, 0)), pl.BlockSpec((1, d), lambda i: (0, 0))],
        out_specs=pl.BlockSpec((tm, d), lambda i: (i, 0)),
        out_shape=jax.ShapeDtypeStruct((m, d), F32),
        compiler_params=_cparams("parallel"),
        name="final_rmsnorm",
    )(x, w.reshape(1, d))


def _gdn_prep_kernel(x_ref, p_ref, st_ref, w_ref, o_ref, *, tiles_per_seq, n_q_blocks, n_qk_blocks):
    i = pl.program_id(0)
    j = pl.program_id(1)
    tr, tc = x_ref.shape
    x = x_ref[...]
    w = w_ref[...]
    first = (i % tiles_per_seq) == 0
    prev8 = jnp.where(first, st_ref[0], p_ref[...])

    def conv_of(rows):
        acc = rows * w[CONV_W - 1:CONV_W]
        for s in range(1, CONV_W):
            acc = acc + pltpu.roll(rows, s, axis=0) * w[CONV_W - 1 - s:CONV_W - s]
        return acc

    def silu(c):
        return c * _sigmoid(c)

    act_main = silu(conv_of(x))
    act_head = silu(conv_of(jnp.concatenate([prev8, x[:SUBLANES]], axis=0))[SUBLANES:])

    @pl.when(j < n_qk_blocks)
    def _():
        scale = jnp.where(j < n_q_blocks, GDN_DK ** -0.5, 1.0)
        for act, nrows in ((act_main, tr), (act_head, SUBLANES)):
            for h in range(tc // LANES):
                a = act[:, h * LANES:(h + 1) * LANES]
                o_ref[:nrows, h * LANES:(h + 1) * LANES] = a * (
                    lax.rsqrt(jnp.sum(a * a, axis=-1, keepdims=True) + EPS) * scale)

    @pl.when(j >= n_qk_blocks)
    def _():
        o_ref[...] = act_main
        o_ref[:SUBLANES, :] = act_head


def gdn_prep(qkvz, conv_state8, w_conv, seq_len, *, tr, tc=1024):
    m = qkvz.shape[0]
    assert seq_len % tr == 0 and tr % SUBLANES == 0
    tiles_per_seq = seq_len // tr
    nq = GDN_QK_HEADS * GDN_DK // tc
    return pl.pallas_call(
        functools.partial(_gdn_prep_kernel, tiles_per_seq=tiles_per_seq, n_q_blocks=nq, n_qk_blocks=2 * nq),
        grid=(m // tr, GDN_QKV // tc),
        in_specs=[pl.BlockSpec((tr, tc), lambda i, j: (i, j)),
                  pl.BlockSpec((SUBLANES, tc), lambda i, j: (jnp.maximum(i * (tr // SUBLANES) - 1, 0), j)),
                  pl.BlockSpec((1, SUBLANES, tc), lambda i, j: (i // tiles_per_seq, 0, j)),
                  pl.BlockSpec((CONV_W, tc), lambda i, j: (0, j))],
        out_specs=pl.BlockSpec((tr, tc), lambda i, j: (i, j)),
        out_shape=jax.ShapeDtypeStruct((m, GDN_QKV), F32),
        compiler_params=_cparams("parallel", "parallel"),
        name="gdn_prep",
    )(qkvz, qkvz, conv_state8, w_conv)


def _gdn_prepare_kernel(q_ref, k_ref, v_ref, ba_ref, gp_ref, w_ref, qe_ref, kd_ref, u_ref, qkd_ref, eg_ref, *, nh):
    hg = pl.program_id(1)
    C = q_ref.shape[0]
    ba = ba_ref[...]
    gp = gp_ref[...]
    beta_t = _sigmoid(ba)
    xs = ba + gp[1:2]
    softplus = jnp.maximum(xs, 0.0) + jnp.log1p(jnp.exp(-jnp.abs(xs)))
    g_t = -jnp.exp(gp[0:1]) * softplus
    row = lax.broadcasted_iota(jnp.int32, (C, C), 0)
    col = lax.broadcasted_iota(jnp.int32, (C, C), 1)
    incl = (row >= col)[None]
    strict = (row > col)[None]
    tri = jnp.where(row >= col, 1.0, 0.0).astype(F32)
    gcum_t = jnp.dot(tri, g_t, precision=HIGHEST, preferred_element_type=F32)
    gcum_r = gcum_t.T
    lane = lax.broadcasted_iota(jnp.int32, (C, LANES), 1)
    sub = lax.broadcasted_iota(jnp.int32, (LANES, C), 0)
    n_levels = int(math.log2(C))
    assert 1 << n_levels == C

    gc_cols, b_cols, gc_rows, qs, ks, vs = [], [], [], [], [], []
    for r in range(nh):
        hv = hg * nh + r
        gc_cols.append(jnp.sum(jnp.where(lane == GDN_V_HEADS + hv, gcum_t, 0.0), axis=1, keepdims=True))
        b_cols.append(jnp.sum(jnp.where(lane == hv, beta_t, 0.0), axis=1, keepdims=True))
        gc_rows.append(jnp.sum(jnp.where(sub == GDN_V_HEADS + hv, gcum_r, 0.0), axis=0, keepdims=True))
        qh = (r * GDN_QK_HEADS) // GDN_V_HEADS
        qs.append(q_ref[:, qh * GDN_DK:(qh + 1) * GDN_DK])
        ks.append(k_ref[:, qh * GDN_DK:(qh + 1) * GDN_DK])
        vs.append(v_ref[:, r * GDN_DV:(r + 1) * GDN_DV])
    gc_col = jnp.stack(gc_cols)
    b_col = jnp.stack(b_cols)
    gc_row = jnp.stack(gc_rows)
    q = jnp.stack(qs)
    k = jnp.stack(ks)
    v = jnp.stack(vs)
    g_last = gc_row[:, :, C - 1:C]
    e_col = jnp.exp(gc_col)
    decay = jnp.exp(jnp.where(incl, gc_col - gc_row, -jnp.inf))
    kb = k.astype(BF16)
    qkk = jnp.einsum("gik,gjk->gij", jnp.concatenate([q.astype(BF16), kb], axis=1), kb,
                     preferred_element_type=F32)
    p = -jnp.where(strict, qkk[:, C:] * decay * b_col, 0.0)
    y = jnp.concatenate([v * b_col, k * (b_col * e_col)], axis=2)
    for lvl in range(n_levels):
        pb = p.astype(BF16)
        y = y + jnp.einsum("gij,gjv->giv", pb, y.astype(BF16), preferred_element_type=F32)
        if lvl + 1 < n_levels:
            p = jnp.einsum("gij,gjk->gik", pb, pb, preferred_element_type=F32)
    u_ref[...] = y[:, :, :GDN_DV]
    w_ref[...] = y[:, :, GDN_DV:].astype(BF16)
    qkd_ref[...] = (qkk[:, :C] * decay).astype(BF16)
    qe_ref[...] = (q * e_col).astype(BF16)
    kd_ref[...] = (k * jnp.exp(g_last - gc_col)).astype(BF16)
    eg_ref[0] = jnp.broadcast_to(jnp.exp(g_last), (nh, 1, LANES))


def gdn_chunk_prepare(conv, ba, gate_params, chunk, *, nh=8):
    m = conv.shape[0]
    nc = m // chunk
    assert nc * chunk == m and GDN_V_HEADS % nh == 0 and nh % 2 == 0
    nqk = nh * GDN_QK_HEADS // GDN_V_HEADS
    k_off = GDN_QK_HEADS // nqk
    v_off = 2 * GDN_QK_HEADS * GDN_DK // (nh * GDN_DV)
    hm = lambda last: pl.BlockSpec((nh, chunk, last), lambda c, hg: (hg, c, 0))
    hshape = lambda last, dt: jax.ShapeDtypeStruct((GDN_V_HEADS, m, last), dt)
    return pl.pallas_call(
        functools.partial(_gdn_prepare_kernel, nh=nh),
        grid=(nc, GDN_V_HEADS // nh),
        in_specs=[pl.BlockSpec((chunk, nqk * GDN_DK), lambda c, hg: (c, hg)),
                  pl.BlockSpec((chunk, nqk * GDN_DK), lambda c, hg: (c, k_off + hg)),
                  pl.BlockSpec((chunk, nh * GDN_DV), lambda c, hg: (c, v_off + hg)),
                  pl.BlockSpec((chunk, LANES), lambda c, hg: (c, 0)),
                  pl.BlockSpec((2, LANES), lambda c, hg: (0, 0))],
        out_specs=[hm(GDN_DV), hm(GDN_DK), hm(GDN_DK), hm(GDN_DV), hm(chunk),
                   pl.BlockSpec((1, nh, 1, LANES), lambda c, hg: (c, hg, 0, 0))],
        out_shape=[hshape(GDN_DV, BF16), hshape(GDN_DK, BF16), hshape(GDN_DK, BF16), hshape(GDN_DV, F32),
                   hshape(chunk, BF16), jax.ShapeDtypeStruct((nc, GDN_V_HEADS, 1, LANES), F32)],
        compiler_params=_cparams("parallel", "parallel"),
        name="gdn_chunk_prepare",
    )(conv, conv, conv, ba, gate_params)


def _gdn_scan_kernel(w_ref, qe_ref, kd_ref, u_ref, qkd_ref, eg_ref, wn_ref, s0_ref, o_ref, s_ref, *, cb):
    nh = w_ref.shape[0]
    C = qkd_ref.shape[2]

    @pl.when(pl.program_id(2) == 0)
    def _():
        s_ref[...] = s0_ref[...]

    s = s_ref[0]
    for ci in range(cb):
        rows = slice(ci * C, (ci + 1) * C)
        sb = s.astype(BF16)
        wq = jnp.concatenate([w_ref[:, rows, :], qe_ref[:, rows, :]], axis=1)
        r = jnp.einsum("hck,hkv->hcv", wq, sb, preferred_element_type=F32)
        vb = (u_ref[:, rows, :] - r[:, :C]).astype(BF16)
        o = r[:, C:] + jnp.einsum("hij,hjv->hiv", qkd_ref[:, rows, :], vb, preferred_element_type=F32)
        s = s * eg_ref[ci] + jnp.einsum("hck,hcv->hkv", kd_ref[:, rows, :], vb, preferred_element_type=F32)
        o = o * lax.rsqrt(jnp.mean(o * o, axis=-1, keepdims=True) + EPS) * wn_ref[...]
        for h in range(nh):
            o_ref[rows, h * GDN_DV:(h + 1) * GDN_DV] = o[h]
    s_ref[0] = s


def gdn_chunk_scan(w, qe, kd, u, qkd, eg, w_norm, s0, seq_len, *, nh=8, cb=4):
    m = u.shape[1]
    chunk = qkd.shape[2]
    b = m // seq_len
    nc = seq_len // chunk
    cb = min(cb, nc)
    assert b * seq_len == m and nc % cb == 0 and GDN_V_HEADS % nh == 0
    steps = nc // cb
    hm = lambda last: pl.BlockSpec((nh, cb * chunk, last), lambda bi, hg, c: (hg, bi * steps + c, 0))
    return pl.pallas_call(
        functools.partial(_gdn_scan_kernel, cb=cb),
        grid=(b, GDN_V_HEADS // nh, steps),
        in_specs=[hm(GDN_DV), hm(GDN_DK), hm(GDN_DK), hm(GDN_DV), hm(chunk),
                  pl.BlockSpec((cb, nh, 1, LANES), lambda bi, hg, c: (bi * steps + c, hg, 0, 0)),
                  pl.BlockSpec((1, GDN_DV), lambda bi, hg, c: (0, 0)),
                  pl.BlockSpec((1, nh, GDN_DK, GDN_DV), lambda bi, hg, c: (bi, hg, 0, 0))],
        out_specs=[pl.BlockSpec((cb * chunk, nh * GDN_DV), lambda bi, hg, c: (bi * steps + c, hg)),
                   pl.BlockSpec((1, nh, GDN_DK, GDN_DV), lambda bi, hg, c: (bi, hg, 0, 0))],
        out_shape=[jax.ShapeDtypeStruct((m, GDN_Z), F32),
                   jax.ShapeDtypeStruct((b, GDN_V_HEADS, GDN_DK, GDN_DV), F32)],
        compiler_params=_cparams("parallel", "parallel", "arbitrary"),
        name="gdn_chunk_scan",
    )(w, qe, kd, u, qkd, eg, w_norm.reshape(1, GDN_DV), s0)


def _toeplitz_bias(trow, rows):
    t = jnp.broadcast_to(trow, (rows, trow.shape[1]))
    return pltpu.roll(t, 0, axis=1, stride=1, stride_axis=0)


def _bias_rows(table):
    assert BAND_PAST == 2 * MAX_REL
    far = table[:, 2 * MAX_REL:]
    left = jnp.broadcast_to(far, (table.shape[0], MAX_REL))
    right = jnp.broadcast_to(far, (table.shape[0], 2 * BAND_PAST - 3 * MAX_REL - 1))
    return jnp.concatenate([left, table[:, ::-1], right], axis=1)[:, None, :]


def _att_prompt_kernel(q_ref, kp_ref, kc_ref, vp_ref, vc_ref, trow_ref, o_ref, bias_ref):
    i = pl.program_id(1)
    qb_rows = q_ref.shape[0]

    @pl.when(i == 0)
    def _():
        r = lax.broadcasted_iota(jnp.int32, bias_ref.shape, 0)
        w = lax.broadcasted_iota(jnp.int32, bias_ref.shape, 1)
        start = r - r % CHUNK
        in_band = jnp.logical_and(w >= start, w < start + BAND)
        bias_ref[...] = jnp.where(in_band, _toeplitz_bias(trow_ref[0], qb_rows), NEG_BIG)

    q = q_ref[...].astype(BF16)
    k = jnp.concatenate([kp_ref[...], kc_ref[...]], axis=0).astype(BF16)
    v = jnp.concatenate([vp_ref[...], vc_ref[...]], axis=0).astype(BF16)
    s = _dot_nt(q, k) * ATT_DH ** -0.5 + bias_ref[...]
    col = lax.broadcasted_iota(jnp.int32, s.shape, 1)
    s = jnp.where(jnp.logical_and(i == 0, col < qb_rows), NEG_BIG, s)
    m = jnp.max(s, axis=-1, keepdims=True)
    p = jnp.exp(s - m)
    l = jnp.sum(p, axis=-1, keepdims=True)
    o_ref[...] = _dot(p.astype(BF16), v) / l


def att_prompt(qkv, trows):
    l = qkv.shape[0]
    qb = BAND_PAST
    assert l % qb == 0
    h = ATT_HEADS
    prev = lambda i: jnp.maximum(i - 1, 0)
    return pl.pallas_call(
        _att_prompt_kernel,
        grid=(h, l // qb),
        in_specs=[pl.BlockSpec((qb, ATT_DH), lambda hh, i: (i, hh)),
                  pl.BlockSpec((qb, ATT_DH), lambda hh, i: (prev(i), h + hh)),
                  pl.BlockSpec((qb, ATT_DH), lambda hh, i: (i, h + hh)),
                  pl.BlockSpec((qb, ATT_DH), lambda hh, i: (prev(i), 2 * h + hh)),
                  pl.BlockSpec((qb, ATT_DH), lambda hh, i: (i, 2 * h + hh)),
                  pl.BlockSpec((1, 1, 2 * qb), lambda hh, i: (hh, 0, 0))],
        out_specs=pl.BlockSpec((qb, ATT_DH), lambda hh, i: (i, hh)),
        out_shape=jax.ShapeDtypeStruct((l, h * ATT_DH), F32),
        scratch_shapes=[pltpu.VMEM((qb, 2 * qb), F32)],
        compiler_params=_cparams("parallel", "arbitrary"),
        name="att_prompt",
    )(qkv, qkv, qkv, qkv, qkv, trows)


def _att_sample_kernel(qkv_ref, ck_ref, cv_ref, trow_ref, o_ref):
    d = ATT_HEADS * ATT_DH
    n_new = qkv_ref.shape[0]
    n_old = ck_ref.shape[1]
    for h in range(ATT_HEADS):
        sl = slice(h * ATT_DH, (h + 1) * ATT_DH)
        q = qkv_ref[:, sl].astype(BF16)
        kn = qkv_ref[:, d + h * ATT_DH:d + (h + 1) * ATT_DH].astype(BF16)
        vn = qkv_ref[:, 2 * d + h * ATT_DH:2 * d + (h + 1) * ATT_DH].astype(BF16)
        ck = ck_ref[0, :, sl].astype(BF16)
        cv = cv_ref[0, :, sl].astype(BF16)
        bias = _toeplitz_bias(trow_ref[h], n_new)
        s1 = _dot_nt(q, ck) * ATT_DH ** -0.5 + bias[:, :n_old]
        s2 = _dot_nt(q, kn) * ATT_DH ** -0.5 + bias[:, n_old:n_old + n_new]
        m = jnp.maximum(jnp.max(s1, axis=-1, keepdims=True), jnp.max(s2, axis=-1, keepdims=True))
        p1 = jnp.exp(s1 - m)
        p2 = jnp.exp(s2 - m)
        l = jnp.sum(p1, axis=-1, keepdims=True) + jnp.sum(p2, axis=-1, keepdims=True)
        o_ref[:, sl] = (_dot(p1.astype(BF16), cv) + _dot(p2.astype(BF16), vn)) / l


def att_sample(qkv, cache_k, cache_v, trows, seq_len):
    m = qkv.shape[0]
    b = m // seq_len
    r = cache_k.shape[1]
    d = ATT_HEADS * ATT_DH
    assert r == BAND_PAST and r + seq_len <= 2 * BAND_PAST
    return pl.pallas_call(
        _att_sample_kernel,
        grid=(b,),
        in_specs=[pl.BlockSpec((seq_len, 3 * d), lambda i: (i, 0)),
                  pl.BlockSpec((1, r, d), lambda i: (i, 0, 0)),
                  pl.BlockSpec((1, r, d), lambda i: (i, 0, 0)),
                  pl.BlockSpec((ATT_HEADS, 1, 2 * BAND_PAST), lambda i: (0, 0, 0))],
        out_specs=pl.BlockSpec((seq_len, d), lambda i: (i, 0)),
        out_shape=jax.ShapeDtypeStruct((m, d), F32),
        compiler_params=_cparams("parallel"),
        name="att_sample",
    )(qkv, cache_k, cache_v, trows)


def _gdn_layer(xp, xs, nw, w_in, w_conv, a_log, dt_bias, w_norm, w_out, state_rec, state_conv, dec_seq):
    w_main = w_in[:, :GDN_QKV + GDN_Z].astype(BF16)
    w_ba = jnp.pad(w_in[:, GDN_QKV + GDN_Z:], ((0, 0), (0, LANES - 2 * GDN_V_HEADS))).astype(BF16)
    w_out_b = w_out.astype(BF16)
    gate_params = jnp.zeros((2, LANES), F32)
    gate_params = gate_params.at[0, GDN_V_HEADS:2 * GDN_V_HEADS].set(a_log)
    gate_params = gate_params.at[1, GDN_V_HEADS:2 * GDN_V_HEADS].set(dt_bias)
    z_blk = GDN_QKV // GDN_Z
    outs = []
    for x, seq_len, chunk, conv0, rec0, tr in (
            (xp, xp.shape[0], CHUNK, jnp.zeros((1, CONV_W - 1, GDN_QKV), F32),
             jnp.zeros((1, GDN_V_HEADS, GDN_DK, GDN_DV), F32), 256),
            (xs, dec_seq, dec_seq, state_conv, state_rec, dec_seq)):
        qkvz, ba = norm_proj(x, nw, w_main, w2=w_ba)
        conv8 = jnp.pad(conv0, ((0, 0), (SUBLANES - (CONV_W - 1), 0), (0, 0)))
        conv = gdn_prep(qkvz, conv8, w_conv, seq_len, tr=tr)
        w, qe, kd, u, qkd, eg = gdn_chunk_prepare(conv, ba, gate_params, chunk)
        o, rec = gdn_chunk_scan(w, qe, kd, u, qkd, eg, w_norm, rec0, seq_len)
        x_new = out_proj(x, o, w_out_b, z_src=qkvz, z_col_block=z_blk, tm=256)
        nb = x.shape[0] // seq_len
        new_conv = qkvz[:, :GDN_QKV].reshape(nb, seq_len, GDN_QKV)[:, seq_len - (CONV_W - 1):]
        outs.append((x_new, new_conv, rec))
    return outs


def _att_layer(xp, xs, nw, w_qkv, b_qkv, table, w_o, b_o, cache_k, cache_v, dec_seq):
    d = xp.shape[1]
    w_qkv_b = w_qkv.astype(BF16)
    w_o_b = w_o.astype(BF16)
    trows = _bias_rows(table)
    qkv_p = norm_proj(xp, nw, w_qkv_b, bias=b_qkv)
    o_p = att_prompt(qkv_p, trows)
    xp_new = out_proj(xp, o_p, w_o_b, bias=b_o)
    lp = xp.shape[0]
    keep = min(BAND_PAST, lp)
    kp = qkv_p[lp - keep:, d:2 * d].reshape(1, keep, ATT_HEADS, ATT_DH)
    vp = qkv_p[lp - keep:, 2 * d:].reshape(1, keep, ATT_HEADS, ATT_DH)
    nb, r = cache_k.shape[0], cache_k.shape[1]
    qkv_s = norm_proj(xs, nw, w_qkv_b, bias=b_qkv)
    o_s = att_sample(qkv_s, cache_k.reshape(nb, r, d), cache_v.reshape(nb, r, d), trows, dec_seq)
    xs_new = out_proj(xs, o_s, w_o_b, bias=b_o)
    ks = qkv_s[:, d:2 * d].reshape(nb, dec_seq, ATT_HEADS, ATT_DH)
    vs = qkv_s[:, 2 * d:].reshape(nb, dec_seq, ATT_HEADS, ATT_DH)
    return (xp_new, kp, vp), (xs_new, ks, vs)


def kernel(x_prompt, x_sample, state_gdn_rec, state_gdn_conv, cache_att_k, cache_att_v, norm_mix, norm_ffn, norm_final, gdn_w_in, gdn_w_conv, gdn_a_log, gdn_dt_bias, gdn_w_norm, gdn_w_out, att_w_qkv, att_b_qkv, att_rel_bias, att_w_o, att_b_o, ffn_w_gate, ffn_w_up, ffn_w_down):
    bp, lp, d = x_prompt.shape
    bs, ls, _ = x_sample.shape
    assert bp == 1
    depth = norm_mix.shape[0]
    xp = x_prompt.reshape(bp * lp, d)
    xs = x_sample.reshape(bs * ls, d)
    p_rec, p_conv, p_k, p_v = [], [], [], []
    s_rec, s_conv, s_k, s_v = [], [], [], []
    for layer in range(depth):
        j = layer // 2
        if layer % 2 == 0:
            (xp, cp, rp), (xs, cs, rs) = _gdn_layer(
                xp, xs, norm_mix[layer], gdn_w_in[j], gdn_w_conv[j], gdn_a_log[j], gdn_dt_bias[j], gdn_w_norm[j],
                gdn_w_out[j], state_gdn_rec[j], state_gdn_conv[j], ls)
            p_conv.append(cp)
            p_rec.append(rp)
            s_conv.append(cs)
            s_rec.append(rs)
        else:
            (xp, kp, vp), (xs, kn, vn) = _att_layer(
                xp, xs, norm_mix[layer], att_w_qkv[j], att_b_qkv[j], att_rel_bias[j], att_w_o[j], att_b_o[j],
                cache_att_k[j], cache_att_v[j], ls)
            p_k.append(kp)
            p_v.append(vp)
            s_k.append(kn)
            s_v.append(vn)
        wg = ffn_w_gate[layer].astype(BF16)
        wu = ffn_w_up[layer].astype(BF16)
        wd = ffn_w_down[layer].astype(BF16)
        xp = ffn(xp, norm_ffn[layer], wg, wu, wd)
        xs = ffn(xs, norm_ffn[layer], wg, wu, wd)
    y_prompt = rmsnorm(xp, norm_final).reshape(bp, lp, d)
    y_sample = rmsnorm(xs, norm_final).reshape(bs, ls, d)
    return (y_prompt, y_sample, jnp.stack(p_rec), jnp.stack(p_conv), jnp.stack(p_k), jnp.stack(p_v),
            jnp.stack(s_rec), jnp.stack(s_conv), jnp.stack(s_k), jnp.stack(s_v))
```

```python
import functools
import math

import jax
import jax.numpy as jnp
from jax import lax
from jax.experimental import pallas as pl
from jax.experimental.pallas import tpu as pltpu

EPS = 1e-6
CHUNK = 64
GDN_QK_HEADS = 16
GDN_V_HEADS = 32
GDN_DK = 128
GDN_DV = 128
CONV_W = 4
GDN_QKV = 2 * GDN_QK_HEADS * GDN_DK + GDN_V_HEADS * GDN_DV
GDN_Z = GDN_V_HEADS * GDN_DV
ATT_HEADS = 16
ATT_DH = 128
BAND_PAST = 512
BAND = BAND_PAST + CHUNK
MAX_REL = 256
LANES = 128
SUBLANES = 8
NEG_BIG = -1e30
VMEM_LIMIT_BYTES = 56 * 1024 * 1024

BF16 = jnp.bfloat16
F32 = jnp.float32
HIGHEST = lax.Precision.HIGHEST


def _cparams(*sem):
    return pltpu.CompilerParams(dimension_semantics=sem, vmem_limit_bytes=VMEM_LIMIT_BYTES)


def _dot(a, b):
    return jnp.dot(a, b, preferred_element_type=F32)


def _dot_nt(a, b):
    return lax.dot_general(a, b, (((1,), (1,)), ((), ())), preferred_element_type=F32)


def _rms(x, w):
    return x * lax.rsqrt(jnp.mean(x * x, axis=-1, keepdims=True) + EPS) * w


def _sigmoid(x):
    return 1.0 / (1.0 + jnp.exp(-x))


def _row_tile(m, pref):
    t = min(m, pref)
    assert m % t == 0, (m, t)
    return t


def _cast_kernel(x_ref, o_ref):
    o_ref[...] = x_ref[0].astype(BF16)


def cast_bf16(w, layer, ncols=None, *, block_bytes=4 * 1024 * 1024):
    _, k, n = w.shape
    ncols = n if ncols is None else ncols
    assert ncols % LANES == 0 or ncols == n
    tk = max(SUBLANES * 2, min(k, block_bytes // (4 * ncols) // 16 * 16))
    while k % tk:
        tk -= 16
    return pl.pallas_call(
        _cast_kernel,
        grid=(k // tk,),
        in_specs=[pl.BlockSpec((1, tk, ncols), lambda i: (layer, i, 0))],
        out_specs=pl.BlockSpec((tk, ncols), lambda i: (i, 0)),
        out_shape=jax.ShapeDtypeStruct((k, ncols), BF16),
        compiler_params=_cparams("parallel"),
        name="cast_bf16",
    )(w)


def _norm_proj_kernel(*refs, has_bias, has_w2):
    x_ref, nw_ref, w_ref = refs[:3]
    pos = 3
    b_ref = w2_ref = None
    if has_bias:
        b_ref = refs[pos]
        pos += 1
    if has_w2:
        w2_ref = refs[pos]
        pos += 1
    o_ref = refs[pos]
    pos += 1
    o2_ref = None
    if has_w2:
        o2_ref = refs[pos]
        pos += 1
    h_ref = refs[pos]

    @pl.when(pl.program_id(1) == 0)
    def _():
        h = _rms(x_ref[...], nw_ref[...]).astype(BF16)
        h_ref[...] = h
        if has_w2:
            o2_ref[...] = _dot(h, w2_ref[...])

    acc = _dot(h_ref[...], w_ref[...])
    if has_bias:
        acc = acc + b_ref[...]
    o_ref[...] = acc


def norm_proj(x, nw, w, bias=None, w2=None, *, tm=512, tn=1024):
    m, d = x.shape
    n = w.shape[1]
    tm = _row_tile(m, tm)
    tn = _row_tile(n, tn)
    in_specs = [pl.BlockSpec((tm, d), lambda i, j: (i, 0)),
                pl.BlockSpec((1, d), lambda i, j: (0, 0)),
                pl.BlockSpec((d, tn), lambda i, j: (0, j))]
    args = [x, nw.reshape(1, d), w]
    if bias is not None:
        in_specs.append(pl.BlockSpec((1, tn), lambda i, j: (0, j)))
        args.append(bias.reshape(1, n))
    out_shape = [jax.ShapeDtypeStruct((m, n), F32)]
    out_specs = [pl.BlockSpec((tm, tn), lambda i, j: (i, j))]
    if w2 is not None:
        n2 = w2.shape[1]
        in_specs.append(pl.BlockSpec((d, n2), lambda i, j: (0, 0)))
        args.append(w2)
        out_shape.append(jax.ShapeDtypeStruct((m, n2), F32))
        out_specs.append(pl.BlockSpec((tm, n2), lambda i, j: (i, 0)))
    res = pl.pallas_call(
        functools.partial(_norm_proj_kernel, has_bias=bias is not None, has_w2=w2 is not None),
        grid=(m // tm, n // tn),
        in_specs=in_specs, out_specs=out_specs, out_shape=out_shape,
        scratch_shapes=[pltpu.VMEM((tm, d), BF16)],
        compiler_params=_cparams("parallel", "arbitrary"),
        name="norm_proj",
    )(*args)
    return res if w2 is not None else res[0]


def _ffn_kernel(x_ref, nw_ref, wg_ref, wu_ref, wd_ref, o_ref, h_ref):
    @pl.when(pl.program_id(1) == 0)
    def _():
        x = x_ref[...]
        h_ref[...] = _rms(x, nw_ref[...]).astype(BF16)
        o_ref[...] = x

    h = h_ref[...]
    g = _dot(h, wg_ref[...])
    u = _dot(h, wu_ref[...])
    a = (g * _sigmoid(g) * u).astype(BF16)
    o_ref[...] += _dot(a, wd_ref[...])


def ffn(x, nw, wg, wu, wd, *, tm=512, tf=512):
    m, d = x.shape
    f = wg.shape[1]
    tm = _row_tile(m, tm)
    tf = _row_tile(f, tf)
    return pl.pallas_call(
        _ffn_kernel,
        grid=(m // tm, f // tf),
        in_specs=[pl.BlockSpec((tm, d), lambda i, j: (i, 0)),
                  pl.BlockSpec((1, d), lambda i, j: (0, 0)),
                  pl.BlockSpec((d, tf), lambda i, j: (0, j)),
                  pl.BlockSpec((d, tf), lambda i, j: (0, j)),
                  pl.BlockSpec((tf, d), lambda i, j: (j, 0))],
        out_specs=pl.BlockSpec((tm, d), lambda i, j: (i, 0)),
        out_shape=jax.ShapeDtypeStruct((m, d), F32),
        scratch_shapes=[pltpu.VMEM((tm, d), BF16)],
        compiler_params=_cparams("parallel", "arbitrary"),
        name="ffn",
    )(x, nw.reshape(1, d), wg, wu, wd)


def _out_proj_kernel(*refs, has_bias):
    a_ref, w_ref = refs[:2]
    b_ref = refs[2] if has_bias else None
    res_ref, o_ref = refs[-2:]
    acc = res_ref[...] + _dot(a_ref[...], w_ref[...])
    if has_bias:
        acc = acc + b_ref[...]
    o_ref[...] = acc


def out_proj(res, a, w, bias=None, *, tm=512, tn=1024):
    m, k = a.shape
    n = w.shape[1]
    assert a.dtype == BF16 and w.dtype == BF16
    tm = _row_tile(m, tm)
    tn = _row_tile(n, tn)
    in_specs = [pl.BlockSpec((tm, k), lambda i, j: (i, 0)),
                pl.BlockSpec((k, tn), lambda i, j: (0, j))]
    args = [a, w]
    if bias is not None:
        in_specs.append(pl.BlockSpec((1, tn), lambda i, j: (0, j)))
        args.append(bias.reshape(1, n))
    in_specs.append(pl.BlockSpec((tm, tn), lambda i, j: (i, j)))
    args.append(res)
    return pl.pallas_call(
        functools.partial(_out_proj_kernel, has_bias=bias is not None),
        grid=(m // tm, n // tn),
        in_specs=in_specs,
        out_specs=pl.BlockSpec((tm, tn), lambda i, j: (i, j)),
        out_shape=jax.ShapeDtypeStruct((m, n), F32),
        compiler_params=_cparams("parallel", "parallel"),
        name="out_proj",
    )(*args)


def _rmsnorm_kernel(x_ref, w_ref, o_ref):
    o_ref[...] = _rms(x_ref[...], w_ref[...])


def rmsnorm(x, w, *, tm=512):
    m, d = x.shape
    tm = _row_tile(m, tm)
    return pl.pallas_call(
        _rmsnorm_kernel,
        grid=(m // tm,),
        in_specs=[pl.BlockSpec((tm, d), lambda i: (i, 0)), pl.BlockSpec((1, d), lambda i: (0, 0))],
        out_specs=pl.BlockSpec((tm, d), lambda i: (i, 0)),
        out_shape=jax.ShapeDtypeStruct((m, d), F32),
        compiler_params=_cparams("parallel"),
        name="final_rmsnorm",
    )(x, w.reshape(1, d))


def _gdn_prep_kernel(x_ref, p_ref, st_ref, w_ref, o_ref, *, tiles_per_seq, n_q_blocks, n_qk_blocks, rb):
    i = pl.program_id(0)
    j = pl.program_id(1)
    tr, tc = x_ref.shape
    w = w_ref[...]
    first = (i % tiles_per_seq) == 0
    prev8 = jnp.where(first, st_ref[0], p_ref[...])

    def strip(halo, body, scale):
        outs = []
        for c in range(tc // LANES):
            sl = slice(c * LANES, (c + 1) * LANES)
            xx = jnp.concatenate([halo[:, sl], body[:, sl]], axis=0)
            acc = xx * w[CONV_W - 1:CONV_W, sl]
            for s in range(1, CONV_W):
                acc = acc + pltpu.roll(xx, s, axis=0) * w[CONV_W - 1 - s:CONV_W - s, sl]
            a = acc[SUBLANES:]
            a = a * _sigmoid(a)
            if scale is not None:
                a = a * (lax.rsqrt(jnp.sum(a * a, axis=-1, keepdims=True) + EPS) * scale)
            outs.append(a)
        return jnp.concatenate(outs, axis=1)

    def run(scale):
        o_ref[:rb, :] = strip(prev8, x_ref[:rb, :], scale)

        def step(b, carry):
            r0 = pl.multiple_of(b * rb, rb)
            o_ref[pl.ds(r0, rb), :] = strip(x_ref[pl.ds(r0 - SUBLANES, SUBLANES), :], x_ref[pl.ds(r0, rb), :], scale)
            return carry

        lax.fori_loop(1, tr // rb, step, 0)

    @pl.when(j < n_qk_blocks)
    def _():
        run(jnp.where(j < n_q_blocks, GDN_DK ** -0.5, 1.0))

    @pl.when(j >= n_qk_blocks)
    def _():
        run(None)


def gdn_prep(qkvz, conv_state8, w_conv, seq_len, *, tr, tc=1024, rb=32):
    m = qkvz.shape[0]
    assert seq_len % tr == 0 and tr % rb == 0 and rb % SUBLANES == 0
    tiles_per_seq = seq_len // tr
    nq = GDN_QK_HEADS * GDN_DK // tc
    return pl.pallas_call(
        functools.partial(_gdn_prep_kernel, tiles_per_seq=tiles_per_seq, n_q_blocks=nq, n_qk_blocks=2 * nq, rb=rb),
        grid=(m // tr, GDN_QKV // tc),
        in_specs=[pl.BlockSpec((tr, tc), lambda i, j: (i, j)),
                  pl.BlockSpec((SUBLANES, tc), lambda i, j: (jnp.maximum(i * (tr // SUBLANES) - 1, 0), j)),
                  pl.BlockSpec((1, SUBLANES, tc), lambda i, j: (i // tiles_per_seq, 0, j)),
                  pl.BlockSpec((CONV_W, tc), lambda i, j: (0, j))],
        out_specs=pl.BlockSpec((tr, tc), lambda i, j: (i, j)),
        out_shape=jax.ShapeDtypeStruct((m, GDN_QKV), F32),
        compiler_params=_cparams("parallel", "parallel"),
        name="gdn_prep",
    )(qkvz, qkvz, conv_state8, w_conv)


def _gdn_prepare_kernel(q_ref, k_ref, v_ref, ba_ref, gp_ref, w_ref, qe_ref, kd_ref, u_ref, qkd_ref, eg_ref, *, nh):
    hg = pl.program_id(1)
    C = q_ref.shape[0]
    ba = ba_ref[...]
    gp = gp_ref[...]
    beta_t = _sigmoid(ba)
    xs = ba + gp[1:2]
    softplus = jnp.maximum(xs, 0.0) + jnp.log1p(jnp.exp(-jnp.abs(xs)))
    g_t = -jnp.exp(gp[0:1]) * softplus
    row = lax.broadcasted_iota(jnp.int32, (C, C), 0)
    col = lax.broadcasted_iota(jnp.int32, (C, C), 1)
    incl = (row >= col)[None]
    strict = (row > col)[None]
    tri = jnp.where(row >= col, 1.0, 0.0).astype(F32)
    gcum_t = jnp.dot(tri, g_t, precision=HIGHEST, preferred_element_type=F32)
    gcum_r = gcum_t.T
    beta_r = beta_t.T
    lane = lax.broadcasted_iota(jnp.int32, (C, LANES), 1)
    sub = lax.broadcasted_iota(jnp.int32, (LANES, C), 0)
    n_levels = int(math.log2(C))
    assert 1 << n_levels == C

    gc_cols, b_cols, gc_rows, b_rows, qs, ks, vs = [], [], [], [], [], [], []
    for r in range(nh):
        hv = hg * nh + r
        gc_cols.append(jnp.sum(jnp.where(lane == GDN_V_HEADS + hv, gcum_t, 0.0), axis=1, keepdims=True))
        b_cols.append(jnp.sum(jnp.where(lane == hv, beta_t, 0.0), axis=1, keepdims=True))
        gc_rows.append(jnp.sum(jnp.where(sub == GDN_V_HEADS + hv, gcum_r, 0.0), axis=0, keepdims=True))
        b_rows.append(jnp.sum(jnp.where(sub == hv, beta_r, 0.0), axis=0, keepdims=True))
        qh = (r * GDN_QK_HEADS) // GDN_V_HEADS
        qs.append(q_ref[:, qh * GDN_DK:(qh + 1) * GDN_DK])
        ks.append(k_ref[:, qh * GDN_DK:(qh + 1) * GDN_DK])
        vs.append(v_ref[:, r * GDN_DV:(r + 1) * GDN_DV])
    gc_col = jnp.stack(gc_cols)
    b_col = jnp.stack(b_cols)
    gc_row = jnp.stack(gc_rows)
    b_row = jnp.stack(b_rows)
    q = jnp.stack(qs)
    k = jnp.stack(ks)
    v = jnp.stack(vs)
    g_last = gc_row[:, :, C - 1:C]
    e_col = jnp.exp(gc_col)
    decay = jnp.exp(jnp.where(incl, gc_col - gc_row, -jnp.inf))
    kb = k.astype(BF16)
    qkk = jnp.einsum("gik,gjk->gij", jnp.concatenate([q.astype(BF16), kb], axis=1), kb,
                     preferred_element_type=F32)
    bmm = functools.partial(jnp.einsum, "gij,gjk->gik", preferred_element_type=F32)

    def split(a):
        hi = a.astype(BF16)
        return hi, (a - hi.astype(F32)).astype(BF16)

    eye = jnp.where(row == col, 1.0, 0.0).astype(F32)[None]
    p0 = -jnp.where(strict, qkk[:, C:] * decay * b_col, 0.0)
    pb = p0.astype(BF16)
    x = eye + p0
    p = bmm(pb, pb)
    for lvl in range(1, n_levels):
        pb = p.astype(BF16)
        if lvl + 1 < n_levels:
            xp = bmm(pb, jnp.concatenate([x, p], axis=2).astype(BF16))
            x = x + xp[:, :, :C]
            p = xp[:, :, C:]
        else:
            x = x + bmm(pb, x.astype(BF16))
    t0 = x.astype(BF16)
    ah, al = split(eye - p0)
    at = bmm(jnp.concatenate([ah, al], axis=1), t0)
    resid = eye - (at[:, :C] + at[:, C:])
    t = t0.astype(F32) + bmm(t0, resid.astype(BF16))
    tb = t * b_row
    tbh, tbl = split(tb)
    vh, vl = split(v)
    uu = bmm(tbh, jnp.concatenate([vh, vl], axis=2))
    u_ref[...] = uu[:, :, :GDN_DV] + uu[:, :, GDN_DV:] + bmm(tbl, vh)
    w_ref[...] = bmm((tb * jnp.exp(gc_row)).astype(BF16), kb).astype(BF16)
    qkd_ref[...] = (qkk[:, :C] * decay).astype(BF16)
    qe_ref[...] = (q * e_col).astype(BF16)
    kd_ref[...] = (k * jnp.exp(g_last - gc_col)).astype(BF16)
    eg_ref[0] = jnp.broadcast_to(jnp.exp(g_last), (nh, 1, LANES))


def gdn_chunk_prepare(conv, ba, gate_params, chunk, *, nh=8):
    m = conv.shape[0]
    nc = m // chunk
    assert nc * chunk == m and GDN_V_HEADS % nh == 0 and nh % 2 == 0
    nqk = nh * GDN_QK_HEADS // GDN_V_HEADS
    k_off = GDN_QK_HEADS // nqk
    v_off = 2 * GDN_QK_HEADS * GDN_DK // (nh * GDN_DV)
    hm = lambda last: pl.BlockSpec((nh, chunk, last), lambda c, hg: (hg, c, 0))
    hshape = lambda last, dt: jax.ShapeDtypeStruct((GDN_V_HEADS, m, last), dt)
    return pl.pallas_call(
        functools.partial(_gdn_prepare_kernel, nh=nh),
        grid=(nc, GDN_V_HEADS // nh),
        in_specs=[pl.BlockSpec((chunk, nqk * GDN_DK), lambda c, hg: (c, hg)),
                  pl.BlockSpec((chunk, nqk * GDN_DK), lambda c, hg: (c, k_off + hg)),
                  pl.BlockSpec((chunk, nh * GDN_DV), lambda c, hg: (c, v_off + hg)),
                  pl.BlockSpec((chunk, LANES), lambda c, hg: (c, 0)),
                  pl.BlockSpec((2, LANES), lambda c, hg: (0, 0))],
        out_specs=[hm(GDN_DV), hm(GDN_DK), hm(GDN_DK), hm(GDN_DV), hm(chunk),
                   pl.BlockSpec((1, nh, 1, LANES), lambda c, hg: (c, hg, 0, 0))],
        out_shape=[hshape(GDN_DV, BF16), hshape(GDN_DK, BF16), hshape(GDN_DK, BF16), hshape(GDN_DV, F32),
                   hshape(chunk, BF16), jax.ShapeDtypeStruct((nc, GDN_V_HEADS, 1, LANES), F32)],
        compiler_params=_cparams("parallel", "parallel"),
        name="gdn_chunk_prepare",
    )(conv, conv, conv, ba, gate_params)


def _gdn_scan_kernel(w_ref, qe_ref, kd_ref, u_ref, qkd_ref, eg_ref, z_ref, wn_ref, s0_ref, o_ref, s_ref, *, cb):
    nh = w_ref.shape[0]
    C = qkd_ref.shape[2]

    @pl.when(pl.program_id(2) == 0)
    def _():
        s_ref[...] = s0_ref[...]

    s = s_ref[0]
    for ci in range(cb):
        rows = slice(ci * C, (ci + 1) * C)
        sb = s.astype(BF16)
        wq = jnp.concatenate([w_ref[:, rows, :], qe_ref[:, rows, :]], axis=1)
        r = jnp.einsum("hck,hkv->hcv", wq, sb, preferred_element_type=F32)
        vb = (u_ref[:, rows, :] - r[:, :C]).astype(BF16)
        o = r[:, C:] + jnp.einsum("hij,hjv->hiv", qkd_ref[:, rows, :], vb, preferred_element_type=F32)
        s = s * eg_ref[ci] + jnp.einsum("hck,hcv->hkv", kd_ref[:, rows, :], vb, preferred_element_type=F32)
        o = o * lax.rsqrt(jnp.mean(o * o, axis=-1, keepdims=True) + EPS) * wn_ref[...]
        for h in range(nh):
            z = z_ref[rows, h * GDN_DV:(h + 1) * GDN_DV]
            o_ref[rows, h * GDN_DV:(h + 1) * GDN_DV] = (o[h] * (z * _sigmoid(z))).astype(BF16)
    s_ref[0] = s


def gdn_chunk_scan(w, qe, kd, u, qkd, eg, qkvz, w_norm, s0, seq_len, *, nh=8, cb=4):
    m = u.shape[1]
    chunk = qkd.shape[2]
    b = m // seq_len
    nc = seq_len // chunk
    cb = min(cb, nc)
    assert b * seq_len == m and nc % cb == 0 and GDN_V_HEADS % nh == 0
    steps = nc // cb
    z_off = GDN_QKV // (nh * GDN_DV)
    hm = lambda last: pl.BlockSpec((nh, cb * chunk, last), lambda bi, hg, c: (hg, bi * steps + c, 0))
    return pl.pallas_call(
        functools.partial(_gdn_scan_kernel, cb=cb),
        grid=(b, GDN_V_HEADS // nh, steps),
        in_specs=[hm(GDN_DV), hm(GDN_DK), hm(GDN_DK), hm(GDN_DV), hm(chunk),
                  pl.BlockSpec((cb, nh, 1, LANES), lambda bi, hg, c: (bi * steps + c, hg, 0, 0)),
                  pl.BlockSpec((cb * chunk, nh * GDN_DV), lambda bi, hg, c: (bi * steps + c, z_off + hg)),
                  pl.BlockSpec((1, GDN_DV), lambda bi, hg, c: (0, 0)),
                  pl.BlockSpec((1, nh, GDN_DK, GDN_DV), lambda bi, hg, c: (bi, hg, 0, 0))],
        out_specs=[pl.BlockSpec((cb * chunk, nh * GDN_DV), lambda bi, hg, c: (bi * steps + c, hg)),
                   pl.BlockSpec((1, nh, GDN_DK, GDN_DV), lambda bi, hg, c: (bi, hg, 0, 0))],
        out_shape=[jax.ShapeDtypeStruct((m, GDN_Z), BF16),
                   jax.ShapeDtypeStruct((b, GDN_V_HEADS, GDN_DK, GDN_DV), F32)],
        compiler_params=_cparams("parallel", "parallel", "arbitrary"),
        name="gdn_chunk_scan",
    )(w, qe, kd, u, qkd, eg, qkvz, w_norm.reshape(1, GDN_DV), s0)


def _toeplitz_bias(trow, rows):
    t = jnp.broadcast_to(trow, (rows, trow.shape[1]))
    return pltpu.roll(t, 0, axis=1, stride=1, stride_axis=0)


def _bias_rows(table):
    assert BAND_PAST == 2 * MAX_REL
    far = table[:, 2 * MAX_REL:]
    left = jnp.broadcast_to(far, (table.shape[0], MAX_REL))
    right = jnp.broadcast_to(far, (table.shape[0], 2 * BAND_PAST - 3 * MAX_REL - 1))
    return jnp.concatenate([left, table[:, ::-1], right], axis=1)[:, None, :]


def _att_prompt_kernel(q_ref, kp_ref, kc_ref, vp_ref, vc_ref, trow_ref, o_ref, bias_ref):
    i = pl.program_id(1)
    qb_rows = q_ref.shape[0]
    hq, hk = bias_ref.shape

    @pl.when(i == 0)
    def _():
        r = lax.broadcasted_iota(jnp.int32, bias_ref.shape, 0)
        w = lax.broadcasted_iota(jnp.int32, bias_ref.shape, 1)
        start = r - r % CHUNK
        in_band = jnp.logical_and(w >= start, w < start + BAND)
        bias_ref[...] = jnp.where(in_band, _toeplitz_bias(trow_ref[0], hq)[:, :hk], NEG_BIG)

    k = jnp.concatenate([kp_ref[...], kc_ref[...]], axis=0).astype(BF16)
    v = jnp.concatenate([vp_ref[...], vc_ref[...]], axis=0).astype(BF16)
    col = lax.broadcasted_iota(jnp.int32, (hq, hk), 1)
    for half in range(qb_rows // hq):
        q = q_ref[half * hq:(half + 1) * hq, :].astype(BF16)
        s = _dot_nt(q, k[half * hq:half * hq + hk]) * ATT_DH ** -0.5 + bias_ref[...]
        s = jnp.where(jnp.logical_and(i == 0, col < qb_rows - half * hq), NEG_BIG, s)
        m = jnp.max(s, axis=-1, keepdims=True)
        p = jnp.exp(s - m)
        l = jnp.sum(p, axis=-1, keepdims=True)
        o_ref[half * hq:(half + 1) * hq, :] = (_dot(p.astype(BF16), v[half * hq:half * hq + hk]) / l).astype(BF16)


def att_prompt(qkv, trows):
    l = qkv.shape[0]
    qb = BAND_PAST
    assert l % qb == 0
    h = ATT_HEADS
    prev = lambda i: jnp.maximum(i - 1, 0)
    return pl.pallas_call(
        _att_prompt_kernel,
        grid=(h, l // qb),
        in_specs=[pl.BlockSpec((qb, ATT_DH), lambda hh, i: (i, hh)),
                  pl.BlockSpec((qb, ATT_DH), lambda hh, i: (prev(i), h + hh)),
                  pl.BlockSpec((qb, ATT_DH), lambda hh, i: (i, h + hh)),
                  pl.BlockSpec((qb, ATT_DH), lambda hh, i: (prev(i), 2 * h + hh)),
                  pl.BlockSpec((qb, ATT_DH), lambda hh, i: (i, 2 * h + hh)),
                  pl.BlockSpec((1, 1, 2 * qb), lambda hh, i: (hh, 0, 0))],
        out_specs=pl.BlockSpec((qb, ATT_DH), lambda hh, i: (i, hh)),
        out_shape=jax.ShapeDtypeStruct((l, h * ATT_DH), BF16),
        scratch_shapes=[pltpu.VMEM((qb // 2, qb // 2 + BAND_PAST), F32)],
        compiler_params=_cparams("parallel", "arbitrary"),
        name="att_prompt",
    )(qkv, qkv, qkv, qkv, qkv, trows)


def _att_sample_kernel(qkv_ref, ck_ref, cv_ref, trow_ref, o_ref):
    d = ATT_HEADS * ATT_DH
    n_new = qkv_ref.shape[0]
    n_old = ck_ref.shape[1]
    for h in range(ATT_HEADS):
        sl = slice(h * ATT_DH, (h + 1) * ATT_DH)
        q = qkv_ref[:, sl].astype(BF16)
        kn = qkv_ref[:, d + h * ATT_DH:d + (h + 1) * ATT_DH].astype(BF16)
        vn = qkv_ref[:, 2 * d + h * ATT_DH:2 * d + (h + 1) * ATT_DH].astype(BF16)
        ck = ck_ref[0, :, sl].astype(BF16)
        cv = cv_ref[0, :, sl].astype(BF16)
        bias = _toeplitz_bias(trow_ref[h], n_new)
        s1 = _dot_nt(q, ck) * ATT_DH ** -0.5 + bias[:, :n_old]
        s2 = _dot_nt(q, kn) * ATT_DH ** -0.5 + bias[:, n_old:n_old + n_new]
        m = jnp.maximum(jnp.max(s1, axis=-1, keepdims=True), jnp.max(s2, axis=-1, keepdims=True))
        p1 = jnp.exp(s1 - m)
        p2 = jnp.exp(s2 - m)
        l = jnp.sum(p1, axis=-1, keepdims=True) + jnp.sum(p2, axis=-1, keepdims=True)
        o_ref[:, sl] = ((_dot(p1.astype(BF16), cv) + _dot(p2.astype(BF16), vn)) / l).astype(BF16)


def att_sample(qkv, cache_k, cache_v, trows, seq_len):
    m = qkv.shape[0]
    b = m // seq_len
    r = cache_k.shape[1]
    d = ATT_HEADS * ATT_DH
    assert r == BAND_PAST and r + seq_len <= 2 * BAND_PAST
    return pl.pallas_call(
        _att_sample_kernel,
        grid=(b,),
        in_specs=[pl.BlockSpec((seq_len, 3 * d), lambda i: (i, 0)),
                  pl.BlockSpec((1, r, d), lambda i: (i, 0, 0)),
                  pl.BlockSpec((1, r, d), lambda i: (i, 0, 0)),
                  pl.BlockSpec((ATT_HEADS, 1, 2 * BAND_PAST), lambda i: (0, 0, 0))],
        out_specs=pl.BlockSpec((seq_len, d), lambda i: (i, 0)),
        out_shape=jax.ShapeDtypeStruct((m, d), BF16),
        compiler_params=_cparams("parallel"),
        name="att_sample",
    )(qkv, cache_k, cache_v, trows)


def _gdn_layer(xp, xs, nw, j, w_in_all, w_conv, a_log, dt_bias, w_norm, w_out_all, state_rec, state_conv, dec_seq):
    w_main = cast_bf16(w_in_all, j, GDN_QKV + GDN_Z)
    w_ba = jnp.pad(w_in_all[j, :, GDN_QKV + GDN_Z:], ((0, 0), (0, LANES - 2 * GDN_V_HEADS))).astype(BF16)
    w_out_b = cast_bf16(w_out_all, j)
    gate_params = jnp.zeros((2, LANES), F32)
    gate_params = gate_params.at[0, GDN_V_HEADS:2 * GDN_V_HEADS].set(a_log)
    gate_params = gate_params.at[1, GDN_V_HEADS:2 * GDN_V_HEADS].set(dt_bias)
    outs = []
    for x, seq_len, chunk, conv0, rec0, tr in (
            (xp, xp.shape[0], CHUNK, jnp.zeros((1, CONV_W - 1, GDN_QKV), F32),
             jnp.zeros((1, GDN_V_HEADS, GDN_DK, GDN_DV), F32), 256),
            (xs, dec_seq, dec_seq, state_conv, state_rec, dec_seq)):
        qkvz, ba = norm_proj(x, nw, w_main, w2=w_ba, tm=1024)
        conv8 = jnp.pad(conv0, ((0, 0), (SUBLANES - (CONV_W - 1), 0), (0, 0)))
        conv = gdn_prep(qkvz, conv8, w_conv, seq_len, tr=tr)
        w, qe, kd, u, qkd, eg = gdn_chunk_prepare(conv, ba, gate_params, chunk)
        o, rec = gdn_chunk_scan(w, qe, kd, u, qkd, eg, qkvz, w_norm, rec0, seq_len)
        x_new = out_proj(x, o, w_out_b)
        nb = x.shape[0] // seq_len
        new_conv = qkvz[:, :GDN_QKV].reshape(nb, seq_len, GDN_QKV)[:, seq_len - (CONV_W - 1):]
        outs.append((x_new, new_conv, rec))
    return outs


def _att_layer(xp, xs, nw, j, w_qkv_all, b_qkv, table, w_o_all, b_o, cache_k, cache_v, dec_seq):
    d = xp.shape[1]
    w_qkv_b = cast_bf16(w_qkv_all, j)
    w_o_b = cast_bf16(w_o_all, j)
    trows = _bias_rows(table)
    qkv_p = norm_proj(xp, nw, w_qkv_b, bias=b_qkv, tm=1024)
    o_p = att_prompt(qkv_p, trows)
    xp_new = out_proj(xp, o_p, w_o_b, bias=b_o, tn=d)
    lp = xp.shape[0]
    keep = min(BAND_PAST, lp)
    kp = qkv_p[lp - keep:, d:2 * d].reshape(1, keep, ATT_HEADS, ATT_DH)
    vp = qkv_p[lp - keep:, 2 * d:].reshape(1, keep, ATT_HEADS, ATT_DH)
    nb, r = cache_k.shape[0], cache_k.shape[1]
    qkv_s = norm_proj(xs, nw, w_qkv_b, bias=b_qkv)
    o_s = att_sample(qkv_s, cache_k.reshape(nb, r, d), cache_v.reshape(nb, r, d), trows, dec_seq)
    xs_new = out_proj(xs, o_s, w_o_b, bias=b_o, tn=d)
    ks = qkv_s[:, d:2 * d].reshape(nb, dec_seq, ATT_HEADS, ATT_DH)
    vs = qkv_s[:, 2 * d:].reshape(nb, dec_seq, ATT_HEADS, ATT_DH)
    return (xp_new, kp, vp), (xs_new, ks, vs)


def kernel(x_prompt, x_sample, state_gdn_rec, state_gdn_conv, cache_att_k, cache_att_v, norm_mix, norm_ffn, norm_final, gdn_w_in, gdn_w_conv, gdn_a_log, gdn_dt_bias, gdn_w_norm, gdn_w_out, att_w_qkv, att_b_qkv, att_rel_bias, att_w_o, att_b_o, ffn_w_gate, ffn_w_up, ffn_w_down):
    bp, lp, d = x_prompt.shape
    bs, ls, _ = x_sample.shape
    assert bp == 1
    depth = norm_mix.shape[0]
    xp = x_prompt.reshape(bp * lp, d)
    xs = x_sample.reshape(bs * ls, d)
    p_rec, p_conv, p_k, p_v = [], [], [], []
    s_rec, s_conv, s_k, s_v = [], [], [], []
    for layer in range(depth):
        j = layer // 2
        if layer % 2 == 0:
            (xp, cp, rp), (xs, cs, rs) = _gdn_layer(
                xp, xs, norm_mix[layer], j, gdn_w_in, gdn_w_conv[j], gdn_a_log[j], gdn_dt_bias[j], gdn_w_norm[j],
                gdn_w_out, state_gdn_rec[j], state_gdn_conv[j], ls)
            p_conv.append(cp)
            p_rec.append(rp)
            s_conv.append(cs)
            s_rec.append(rs)
        else:
            (xp, kp, vp), (xs, kn, vn) = _att_layer(
                xp, xs, norm_mix[layer], j, att_w_qkv, att_b_qkv[j], att_rel_bias[j], att_w_o, att_b_o[j],
                cache_att_k[j], cache_att_v[j], ls)
            p_k.append(kp)
            p_v.append(vp)
            s_k.append(kn)
            s_v.append(vn)
        wg = cast_bf16(ffn_w_gate, layer)
        wu = cast_bf16(ffn_w_up, layer)
        wd = cast_bf16(ffn_w_down, layer)
        xp = ffn(xp, norm_ffn[layer], wg, wu, wd)
        xs = ffn(xs, norm_ffn[layer], wg, wu, wd)
    y_prompt = rmsnorm(xp, norm_final).reshape(bp, lp, d)
    y_sample = rmsnorm(xs, norm_final).reshape(bs, ls, d)
    return (y_prompt, y_sample, jnp.stack(p_rec), jnp.stack(p_conv), jnp.stack(p_k), jnp.stack(p_v),
            jnp.stack(s_rec), jnp.stack(s_conv), jnp.stack(s_k), jnp.stack(s_v))
```

```python
import functools
import math

import jax
import jax.numpy as jnp
from jax import lax
from jax.experimental import pallas as pl
from jax.experimental.pallas import tpu as pltpu

EPS = 1e-6
CHUNK = 64
GDN_QK_HEADS = 16
GDN_V_HEADS = 32
GDN_DK = 128
GDN_DV = 128
CONV_W = 4
GDN_QKV = 2 * GDN_QK_HEADS * GDN_DK + GDN_V_HEADS * GDN_DV
GDN_Z = GDN_V_HEADS * GDN_DV
ATT_HEADS = 16
ATT_DH = 128
BAND_PAST = 512
BAND = BAND_PAST + CHUNK
MAX_REL = 256
LANES = 128
SUBLANES = 8
NEG_BIG = -1e30
VMEM_LIMIT_BYTES = 56 * 1024 * 1024

BF16 = jnp.bfloat16
F32 = jnp.float32
HIGHEST = lax.Precision.HIGHEST


def _cparams(*sem):
    return pltpu.CompilerParams(dimension_semantics=sem, vmem_limit_bytes=VMEM_LIMIT_BYTES)


def _dot(a, b):
    return jnp.dot(a, b, preferred_element_type=F32)


def _dot_nt(a, b):
    return lax.dot_general(a, b, (((1,), (1,)), ((), ())), preferred_element_type=F32)


def _rms(x, w):
    return x * lax.rsqrt(jnp.mean(x * x, axis=-1, keepdims=True) + EPS) * w


def _sigmoid(x):
    return 1.0 / (1.0 + jnp.exp(-x))


def _row_tile(m, pref):
    t = min(m, pref)
    assert m % t == 0, (m, t)
    return t


def _cast_kernel(x_ref, o_ref):
    o_ref[...] = x_ref[0].astype(BF16)


def cast_bf16(w, layer, ncols=None, *, block_bytes=4 * 1024 * 1024):
    _, k, n = w.shape
    ncols = n if ncols is None else ncols
    assert ncols % LANES == 0 or ncols == n
    tk = max(SUBLANES * 2, min(k, block_bytes // (4 * ncols) // 16 * 16))
    while k % tk:
        tk -= 16
    return pl.pallas_call(
        _cast_kernel,
        grid=(k // tk,),
        in_specs=[pl.BlockSpec((1, tk, ncols), lambda i: (layer, i, 0))],
        out_specs=pl.BlockSpec((tk, ncols), lambda i: (i, 0)),
        out_shape=jax.ShapeDtypeStruct((k, ncols), BF16),
        compiler_params=_cparams("parallel"),
        name="cast_bf16",
    )(w)


def _norm_proj_kernel(*refs, has_bias, has_w2):
    x_ref, nw_ref, w_ref = refs[:3]
    pos = 3
    b_ref = w2_ref = None
    if has_bias:
        b_ref = refs[pos]
        pos += 1
    if has_w2:
        w2_ref = refs[pos]
        pos += 1
    o_ref = refs[pos]
    pos += 1
    o2_ref = None
    if has_w2:
        o2_ref = refs[pos]
        pos += 1
    h_ref = refs[pos]

    @pl.when(pl.program_id(1) == 0)
    def _():
        h = _rms(x_ref[...], nw_ref[...]).astype(BF16)
        h_ref[...] = h
        if has_w2:
            o2_ref[...] = _dot(h, w2_ref[...])

    acc = _dot(h_ref[...], w_ref[...])
    if has_bias:
        acc = acc + b_ref[...]
    o_ref[...] = acc


def norm_proj(x, nw, w, bias=None, w2=None, *, tm=512, tn=1024):
    m, d = x.shape
    n = w.shape[1]
    tm = _row_tile(m, tm)
    tn = _row_tile(n, tn)
    in_specs = [pl.BlockSpec((tm, d), lambda i, j: (i, 0)),
                pl.BlockSpec((1, d), lambda i, j: (0, 0)),
                pl.BlockSpec((d, tn), lambda i, j: (0, j))]
    args = [x, nw.reshape(1, d), w]
    if bias is not None:
        in_specs.append(pl.BlockSpec((1, tn), lambda i, j: (0, j)))
        args.append(bias.reshape(1, n))
    out_shape = [jax.ShapeDtypeStruct((m, n), F32)]
    out_specs = [pl.BlockSpec((tm, tn), lambda i, j: (i, j))]
    if w2 is not None:
        n2 = w2.shape[1]
        in_specs.append(pl.BlockSpec((d, n2), lambda i, j: (0, 0)))
        args.append(w2)
        out_shape.append(jax.ShapeDtypeStruct((m, n2), F32))
        out_specs.append(pl.BlockSpec((tm, n2), lambda i, j: (i, 0)))
    res = pl.pallas_call(
        functools.partial(_norm_proj_kernel, has_bias=bias is not None, has_w2=w2 is not None),
        grid=(m // tm, n // tn),
        in_specs=in_specs, out_specs=out_specs, out_shape=out_shape,
        scratch_shapes=[pltpu.VMEM((tm, d), BF16)],
        compiler_params=_cparams("parallel", "arbitrary"),
        name="norm_proj",
    )(*args)
    return res if w2 is not None else res[0]


def _ffn_kernel(x_ref, nw_ref, wg_ref, wu_ref, wd_ref, o_ref, h_ref):
    @pl.when(pl.program_id(1) == 0)
    def _():
        x = x_ref[...]
        h_ref[...] = _rms(x, nw_ref[...]).astype(BF16)
        o_ref[...] = x

    h = h_ref[...]
    g = _dot(h, wg_ref[...])
    u = _dot(h, wu_ref[...])
    a = (g * _sigmoid(g) * u).astype(BF16)
    o_ref[...] += _dot(a, wd_ref[...])


def ffn(x, nw, wg, wu, wd, *, tm=512, tf=512):
    m, d = x.shape
    f = wg.shape[1]
    tm = _row_tile(m, tm)
    tf = _row_tile(f, tf)
    return pl.pallas_call(
        _ffn_kernel,
        grid=(m // tm, f // tf),
        in_specs=[pl.BlockSpec((tm, d), lambda i, j: (i, 0)),
                  pl.BlockSpec((1, d), lambda i, j: (0, 0)),
                  pl.BlockSpec((d, tf), lambda i, j: (0, j)),
                  pl.BlockSpec((d, tf), lambda i, j: (0, j)),
                  pl.BlockSpec((tf, d), lambda i, j: (j, 0))],
        out_specs=pl.BlockSpec((tm, d), lambda i, j: (i, 0)),
        out_shape=jax.ShapeDtypeStruct((m, d), F32),
        scratch_shapes=[pltpu.VMEM((tm, d), BF16)],
        compiler_params=_cparams("parallel", "arbitrary"),
        name="ffn",
    )(x, nw.reshape(1, d), wg, wu, wd)


def _out_proj_kernel(*refs, has_bias):
    a_ref, w_ref = refs[:2]
    b_ref = refs[2] if has_bias else None
    res_ref, o_ref = refs[-2:]
    acc = res_ref[...] + _dot(a_ref[...], w_ref[...])
    if has_bias:
        acc = acc + b_ref[...]
    o_ref[...] = acc


def out_proj(res, a, w, bias=None, *, tm=512, tn=1024):
    m, k = a.shape
    n = w.shape[1]
    assert a.dtype == BF16 and w.dtype == BF16
    tm = _row_tile(m, tm)
    tn = _row_tile(n, tn)
    in_specs = [pl.BlockSpec((tm, k), lambda i, j: (i, 0)),
                pl.BlockSpec((k, tn), lambda i, j: (0, j))]
    args = [a, w]
    if bias is not None:
        in_specs.append(pl.BlockSpec((1, tn), lambda i, j: (0, j)))
        args.append(bias.reshape(1, n))
    in_specs.append(pl.BlockSpec((tm, tn), lambda i, j: (i, j)))
    args.append(res)
    return pl.pallas_call(
        functools.partial(_out_proj_kernel, has_bias=bias is not None),
        grid=(m // tm, n // tn),
        in_specs=in_specs,
        out_specs=pl.BlockSpec((tm, tn), lambda i, j: (i, j)),
        out_shape=jax.ShapeDtypeStruct((m, n), F32),
        compiler_params=_cparams("parallel", "parallel"),
        name="out_proj",
    )(*args)


def _rmsnorm_kernel(x_ref, w_ref, o_ref):
    o_ref[...] = _rms(x_ref[...], w_ref[...])


def rmsnorm(x, w, *, tm=512):
    m, d = x.shape
    tm = _row_tile(m, tm)
    return pl.pallas_call(
        _rmsnorm_kernel,
        grid=(m // tm,),
        in_specs=[pl.BlockSpec((tm, d), lambda i: (i, 0)), pl.BlockSpec((1, d), lambda i: (0, 0))],
        out_specs=pl.BlockSpec((tm, d), lambda i: (i, 0)),
        out_shape=jax.ShapeDtypeStruct((m, d), F32),
        compiler_params=_cparams("parallel"),
        name="final_rmsnorm",
    )(x, w.reshape(1, d))


def _gdn_prep_kernel(x_ref, p_ref, st_ref, w_ref, o_ref, *, tiles_per_seq, n_q_blocks, n_qk_blocks, rb):
    i = pl.program_id(0)
    j = pl.program_id(1)
    tr, tc = x_ref.shape
    w = w_ref[...]
    first = (i % tiles_per_seq) == 0
    prev8 = jnp.where(first, st_ref[0], p_ref[...])

    def strip(halo, body, scale):
        outs = []
        for c in range(tc // LANES):
            sl = slice(c * LANES, (c + 1) * LANES)
            xx = jnp.concatenate([halo[:, sl], body[:, sl]], axis=0)
            acc = xx * w[CONV_W - 1:CONV_W, sl]
            for s in range(1, CONV_W):
                acc = acc + pltpu.roll(xx, s, axis=0) * w[CONV_W - 1 - s:CONV_W - s, sl]
            a = acc[SUBLANES:]
            a = a * _sigmoid(a)
            if scale is not None:
                a = a * (lax.rsqrt(jnp.sum(a * a, axis=-1, keepdims=True) + EPS) * scale)
            outs.append(a)
        return jnp.concatenate(outs, axis=1)

    def run(scale):
        o_ref[:rb, :] = strip(prev8, x_ref[:rb, :], scale)

        def step(b, carry):
            r0 = pl.multiple_of(b * rb, rb)
            o_ref[pl.ds(r0, rb), :] = strip(x_ref[pl.ds(r0 - SUBLANES, SUBLANES), :], x_ref[pl.ds(r0, rb), :], scale)
            return carry

        lax.fori_loop(1, tr // rb, step, 0)

    @pl.when(j < n_qk_blocks)
    def _():
        run(jnp.where(j < n_q_blocks, GDN_DK ** -0.5, 1.0))

    @pl.when(j >= n_qk_blocks)
    def _():
        run(None)


def gdn_prep(qkvz, conv_state8, w_conv, seq_len, *, tr, tc=1024, rb=32):
    m = qkvz.shape[0]
    assert seq_len % tr == 0 and tr % rb == 0 and rb % SUBLANES == 0
    tiles_per_seq = seq_len // tr
    nq = GDN_QK_HEADS * GDN_DK // tc
    return pl.pallas_call(
        functools.partial(_gdn_prep_kernel, tiles_per_seq=tiles_per_seq, n_q_blocks=nq, n_qk_blocks=2 * nq, rb=rb),
        grid=(m // tr, GDN_QKV // tc),
        in_specs=[pl.BlockSpec((tr, tc), lambda i, j: (i, j)),
                  pl.BlockSpec((SUBLANES, tc), lambda i, j: (jnp.maximum(i * (tr // SUBLANES) - 1, 0), j)),
                  pl.BlockSpec((1, SUBLANES, tc), lambda i, j: (i // tiles_per_seq, 0, j)),
                  pl.BlockSpec((CONV_W, tc), lambda i, j: (0, j))],
        out_specs=pl.BlockSpec((tr, tc), lambda i, j: (i, j)),
        out_shape=jax.ShapeDtypeStruct((m, GDN_QKV), F32),
        compiler_params=_cparams("parallel", "parallel"),
        name="gdn_prep",
    )(qkvz, qkvz, conv_state8, w_conv)


def _gdn_gates_kernel(ba_ref, gp_ref, t_ref, r_ref, *, chunk):
    n = ba_ref.shape[0] // chunk
    gp = gp_ref[...]
    row = lax.broadcasted_iota(jnp.int32, (chunk, chunk), 0)
    col = lax.broadcasted_iota(jnp.int32, (chunk, chunk), 1)
    tri = jnp.where(row >= col, 1.0, 0.0).astype(F32)
    lane = lax.broadcasted_iota(jnp.int32, (chunk, LANES), 1)
    for ci in range(n):
        rows = slice(ci * chunk, (ci + 1) * chunk)
        ba = ba_ref[rows, :]
        xs = ba + gp[1:2]
        softplus = jnp.maximum(xs, 0.0) + jnp.log1p(jnp.exp(-jnp.abs(xs)))
        g = -jnp.exp(gp[0:1]) * softplus
        gcum = jnp.dot(tri, g, precision=HIGHEST, preferred_element_type=F32)
        t = jnp.where(lane < GDN_V_HEADS, _sigmoid(ba), gcum)
        t_ref[rows, :] = t
        r_ref[ci] = t.T


def gdn_gates(ba, gate_params, chunk, *, chunks_per_step=8):
    m = ba.shape[0]
    nc = m // chunk
    n = min(chunks_per_step, nc)
    assert nc * chunk == m and nc % n == 0
    return pl.pallas_call(
        functools.partial(_gdn_gates_kernel, chunk=chunk),
        grid=(nc // n,),
        in_specs=[pl.BlockSpec((n * chunk, LANES), lambda i: (i, 0)),
                  pl.BlockSpec((2, LANES), lambda i: (0, 0))],
        out_specs=[pl.BlockSpec((n * chunk, LANES), lambda i: (i, 0)),
                   pl.BlockSpec((n, LANES, chunk), lambda i: (i, 0, 0))],
        out_shape=[jax.ShapeDtypeStruct((m, LANES), F32), jax.ShapeDtypeStruct((nc, LANES, chunk), F32)],
        compiler_params=_cparams("parallel"),
        name="gdn_gates",
    )(ba, gate_params)


def _gdn_prepare_kernel(q_ref, k_ref, v_ref, t_ref, r_ref, w_ref, qe_ref, kd_ref, u_ref, qkd_ref, eg_ref):
    C = q_ref.shape[0]
    nh = GDN_V_HEADS
    row = lax.broadcasted_iota(jnp.int32, (C, C), 0)
    col = lax.broadcasted_iota(jnp.int32, (C, C), 1)
    incl = (row >= col)[None]
    off_diag = (row != col)[None]
    eye = jnp.where(row == col, 1.0, 0.0).astype(F32)[None]
    n_levels = int(math.log2(C))
    assert 1 << n_levels == C
    bmm = functools.partial(jnp.einsum, "gij,gjk->gik", preferred_element_type=F32)

    def split(a):
        hi = a.astype(BF16)
        return hi, (a - hi.astype(F32)).astype(BF16)

    t_cols = t_ref[...]
    t_rows = r_ref[0]
    b_col = jnp.stack([t_cols[:, r:r + 1] for r in range(nh)])
    gc_col = jnp.stack([t_cols[:, nh + r:nh + r + 1] for r in range(nh)])
    b_row = jnp.stack([t_rows[r:r + 1, :] for r in range(nh)])
    gc_row = jnp.stack([t_rows[nh + r:nh + r + 1, :] for r in range(nh)])
    qk_heads = [(r * GDN_QK_HEADS) // GDN_V_HEADS for r in range(nh)]
    q = jnp.stack([q_ref[:, h * GDN_DK:(h + 1) * GDN_DK] for h in qk_heads])
    k = jnp.stack([k_ref[:, h * GDN_DK:(h + 1) * GDN_DK] for h in qk_heads])
    v = jnp.stack([v_ref[:, r * GDN_DV:(r + 1) * GDN_DV] for r in range(nh)])
    g_last = gc_row[:, :, C - 1:C]
    decay = jnp.exp(jnp.where(incl, gc_col - gc_row, -jnp.inf))
    kb = k.astype(BF16)
    qkk = jnp.einsum("gik,gjk->gij", jnp.concatenate([q.astype(BF16), kb], axis=1), kb,
                     preferred_element_type=F32)
    p0 = -jnp.where(off_diag, qkk[:, C:] * decay * b_col, 0.0)
    pb = p0.astype(BF16)
    x = eye + p0
    p = bmm(pb, pb)
    for lvl in range(1, n_levels):
        pb = p.astype(BF16)
        if lvl + 1 < n_levels:
            xp = bmm(pb, jnp.concatenate([x, p], axis=2).astype(BF16))
            x = x + xp[:, :, :C]
            p = xp[:, :, C:]
        else:
            x = x + bmm(pb, x.astype(BF16))
    t0 = x.astype(BF16)
    ah, al = split(eye - p0)
    at = bmm(jnp.concatenate([ah, al], axis=1), t0)
    resid = eye - (at[:, :C] + at[:, C:])
    t = t0.astype(F32) + bmm(t0, resid.astype(BF16))
    tb = t * b_row
    tbh, tbl = split(tb)
    vh, vl = split(v)
    uu = bmm(tbh, jnp.concatenate([vh, vl], axis=2))
    u_ref[...] = uu[:, :, :GDN_DV] + uu[:, :, GDN_DV:] + bmm(tbl, vh)
    w_ref[...] = bmm((tb * jnp.exp(gc_row)).astype(BF16), kb).astype(BF16)
    qkd_ref[...] = (qkk[:, :C] * decay).astype(BF16)
    qe_ref[...] = (q * jnp.exp(gc_col)).astype(BF16)
    kd_ref[...] = (k * jnp.exp(g_last - gc_col)).astype(BF16)
    eg_ref[0] = jnp.broadcast_to(jnp.exp(g_last), (nh, 1, LANES))


def gdn_chunk_prepare(conv, gates_t, gates_r, chunk):
    m = conv.shape[0]
    nc = m // chunk
    assert nc * chunk == m
    nqk = GDN_QK_HEADS * GDN_DK
    hm = lambda last: pl.BlockSpec((GDN_V_HEADS, chunk, last), lambda c: (0, c, 0))
    hshape = lambda last, dt: jax.ShapeDtypeStruct((GDN_V_HEADS, m, last), dt)
    return pl.pallas_call(
        _gdn_prepare_kernel,
        grid=(nc,),
        in_specs=[pl.BlockSpec((chunk, nqk), lambda c: (c, 0)),
                  pl.BlockSpec((chunk, nqk), lambda c: (c, 1)),
                  pl.BlockSpec((chunk, GDN_Z), lambda c: (c, 2 * nqk // GDN_Z)),
                  pl.BlockSpec((chunk, LANES), lambda c: (c, 0)),
                  pl.BlockSpec((1, LANES, chunk), lambda c: (c, 0, 0))],
        out_specs=[hm(GDN_DV), hm(GDN_DK), hm(GDN_DK), hm(GDN_DV), hm(chunk),
                   pl.BlockSpec((1, GDN_V_HEADS, 1, LANES), lambda c: (c, 0, 0, 0))],
        out_shape=[hshape(GDN_DV, BF16), hshape(GDN_DK, BF16), hshape(GDN_DK, BF16), hshape(GDN_DV, F32),
                   hshape(chunk, BF16), jax.ShapeDtypeStruct((nc, GDN_V_HEADS, 1, LANES), F32)],
        compiler_params=_cparams("parallel"),
        name="gdn_chunk_prepare",
    )(conv, conv, conv, gates_t, gates_r)


def _gdn_scan_kernel(w_ref, qe_ref, kd_ref, u_ref, qkd_ref, eg_ref, z_ref, wn_ref, s0_ref, o_ref, s_ref, *, cb):
    nh = w_ref.shape[0]
    C = qkd_ref.shape[2]

    @pl.when(pl.program_id(2) == 0)
    def _():
        s_ref[...] = s0_ref[0]

    s = s_ref[0]
    for ci in range(cb):
        rows = slice(ci * C, (ci + 1) * C)
        sb = s.astype(BF16)
        wq = jnp.concatenate([w_ref[:, rows, :], qe_ref[:, rows, :]], axis=1)
        r = jnp.einsum("hck,hkv->hcv", wq, sb, preferred_element_type=F32)
        vb = (u_ref[:, rows, :] - r[:, :C]).astype(BF16)
        o = r[:, C:] + jnp.einsum("hij,hjv->hiv", qkd_ref[:, rows, :], vb, preferred_element_type=F32)
        s = s * eg_ref[ci] + jnp.einsum("hck,hcv->hkv", kd_ref[:, rows, :], vb, preferred_element_type=F32)
        o = o * lax.rsqrt(jnp.mean(o * o, axis=-1, keepdims=True) + EPS) * wn_ref[...]
        for h in range(nh):
            z = z_ref[rows, h * GDN_DV:(h + 1) * GDN_DV]
            o_ref[rows, h * GDN_DV:(h + 1) * GDN_DV] = (o[h] * (z * _sigmoid(z))).astype(BF16)
    s_ref[0] = s


def gdn_chunk_scan(w, qe, kd, u, qkd, eg, qkvz, w_norm, s0, layer, seq_len, *, nh=8, cb=4):
    m = u.shape[1]
    chunk = qkd.shape[2]
    b = m // seq_len
    nc = seq_len // chunk
    cb = min(cb, nc)
    assert b * seq_len == m and nc % cb == 0 and GDN_V_HEADS % nh == 0
    steps = nc // cb
    z_off = GDN_QKV // (nh * GDN_DV)
    hm = lambda last: pl.BlockSpec((nh, cb * chunk, last), lambda bi, hg, c: (hg, bi * steps + c, 0))
    return pl.pallas_call(
        functools.partial(_gdn_scan_kernel, cb=cb),
        grid=(b, GDN_V_HEADS // nh, steps),
        in_specs=[hm(GDN_DV), hm(GDN_DK), hm(GDN_DK), hm(GDN_DV), hm(chunk),
                  pl.BlockSpec((cb, nh, 1, LANES), lambda bi, hg, c: (bi * steps + c, hg, 0, 0)),
                  pl.BlockSpec((cb * chunk, nh * GDN_DV), lambda bi, hg, c: (bi * steps + c, z_off + hg)),
                  pl.BlockSpec((1, GDN_DV), lambda bi, hg, c: (0, 0)),
                  pl.BlockSpec((1, 1, nh, GDN_DK, GDN_DV), lambda bi, hg, c: (layer, bi, hg, 0, 0))],
        out_specs=[pl.BlockSpec((cb * chunk, nh * GDN_DV), lambda bi, hg, c: (bi * steps + c, hg)),
                   pl.BlockSpec((1, nh, GDN_DK, GDN_DV), lambda bi, hg, c: (bi, hg, 0, 0))],
        out_shape=[jax.ShapeDtypeStruct((m, GDN_Z), BF16),
                   jax.ShapeDtypeStruct((b, GDN_V_HEADS, GDN_DK, GDN_DV), F32)],
        compiler_params=_cparams("parallel", "parallel", "arbitrary"),
        name="gdn_chunk_scan",
    )(w, qe, kd, u, qkd, eg, qkvz, w_norm.reshape(1, GDN_DV), s0)


def _toeplitz_bias(trow, rows):
    t = jnp.broadcast_to(trow, (rows, trow.shape[1]))
    return pltpu.roll(t, 0, axis=1, stride=1, stride_axis=0)


def _bias_rows(table):
    assert BAND_PAST == 2 * MAX_REL
    far = table[:, 2 * MAX_REL:]
    left = jnp.broadcast_to(far, (table.shape[0], MAX_REL))
    right = jnp.broadcast_to(far, (table.shape[0], 2 * BAND_PAST - 3 * MAX_REL - 1))
    return jnp.concatenate([left, table[:, ::-1], right], axis=1)[:, None, :]


def _att_prompt_kernel(q_ref, kp_ref, kc_ref, vp_ref, vc_ref, trow_ref, o_ref, bias_ref):
    i = pl.program_id(1)
    qb_rows = q_ref.shape[0]
    hq, hk = bias_ref.shape

    @pl.when(i == 0)
    def _():
        r = lax.broadcasted_iota(jnp.int32, bias_ref.shape, 0)
        w = lax.broadcasted_iota(jnp.int32, bias_ref.shape, 1)
        start = r - r % CHUNK
        in_band = jnp.logical_and(w >= start, w < start + BAND)
        bias_ref[...] = jnp.where(in_band, _toeplitz_bias(trow_ref[0], hq)[:, :hk], NEG_BIG)

    k = jnp.concatenate([kp_ref[...], kc_ref[...]], axis=0).astype(BF16)
    v = jnp.concatenate([vp_ref[...], vc_ref[...]], axis=0).astype(BF16)
    col = lax.broadcasted_iota(jnp.int32, (hq, hk), 1)
    for half in range(qb_rows // hq):
        q = q_ref[half * hq:(half + 1) * hq, :].astype(BF16)
        s = _dot_nt(q, k[half * hq:half * hq + hk]) * ATT_DH ** -0.5 + bias_ref[...]
        s = jnp.where(jnp.logical_and(i == 0, col < qb_rows - half * hq), NEG_BIG, s)
        m = jnp.max(s, axis=-1, keepdims=True)
        p = jnp.exp(s - m)
        l = jnp.sum(p, axis=-1, keepdims=True)
        o_ref[half * hq:(half + 1) * hq, :] = (_dot(p.astype(BF16), v[half * hq:half * hq + hk]) / l).astype(BF16)


def att_prompt(qkv, trows):
    l = qkv.shape[0]
    qb = BAND_PAST
    assert l % qb == 0
    h = ATT_HEADS
    prev = lambda i: jnp.maximum(i - 1, 0)
    return pl.pallas_call(
        _att_prompt_kernel,
        grid=(h, l // qb),
        in_specs=[pl.BlockSpec((qb, ATT_DH), lambda hh, i: (i, hh)),
                  pl.BlockSpec((qb, ATT_DH), lambda hh, i: (prev(i), h + hh)),
                  pl.BlockSpec((qb, ATT_DH), lambda hh, i: (i, h + hh)),
                  pl.BlockSpec((qb, ATT_DH), lambda hh, i: (prev(i), 2 * h + hh)),
                  pl.BlockSpec((qb, ATT_DH), lambda hh, i: (i, 2 * h + hh)),
                  pl.BlockSpec((1, 1, 2 * qb), lambda hh, i: (hh, 0, 0))],
        out_specs=pl.BlockSpec((qb, ATT_DH), lambda hh, i: (i, hh)),
        out_shape=jax.ShapeDtypeStruct((l, h * ATT_DH), BF16),
        scratch_shapes=[pltpu.VMEM((qb // 2, qb // 2 + BAND_PAST), F32)],
        compiler_params=_cparams("parallel", "arbitrary"),
        name="att_prompt",
    )(qkv, qkv, qkv, qkv, qkv, trows)


def _att_sample_kernel(qkv_ref, ck_ref, cv_ref, trow_ref, o_ref):
    d = ATT_HEADS * ATT_DH
    n_new = qkv_ref.shape[0]
    n_old = ck_ref.shape[2]
    for h in range(ATT_HEADS):
        sl = slice(h * ATT_DH, (h + 1) * ATT_DH)
        q = qkv_ref[:, sl].astype(BF16)
        kn = qkv_ref[:, d + h * ATT_DH:d + (h + 1) * ATT_DH].astype(BF16)
        vn = qkv_ref[:, 2 * d + h * ATT_DH:2 * d + (h + 1) * ATT_DH].astype(BF16)
        ck = ck_ref[0, 0, :, sl].astype(BF16)
        cv = cv_ref[0, 0, :, sl].astype(BF16)
        bias = _toeplitz_bias(trow_ref[h], n_new)
        s1 = _dot_nt(q, ck) * ATT_DH ** -0.5 + bias[:, :n_old]
        s2 = _dot_nt(q, kn) * ATT_DH ** -0.5 + bias[:, n_old:n_old + n_new]
        m = jnp.maximum(jnp.max(s1, axis=-1, keepdims=True), jnp.max(s2, axis=-1, keepdims=True))
        p1 = jnp.exp(s1 - m)
        p2 = jnp.exp(s2 - m)
        l = jnp.sum(p1, axis=-1, keepdims=True) + jnp.sum(p2, axis=-1, keepdims=True)
        o_ref[:, sl] = ((_dot(p1.astype(BF16), cv) + _dot(p2.astype(BF16), vn)) / l).astype(BF16)


def att_sample(qkv, cache_k, cache_v, layer, trows, seq_len):
    m = qkv.shape[0]
    b = m // seq_len
    r = cache_k.shape[2]
    d = ATT_HEADS * ATT_DH
    assert r == BAND_PAST and r + seq_len <= 2 * BAND_PAST
    return pl.pallas_call(
        _att_sample_kernel,
        grid=(b,),
        in_specs=[pl.BlockSpec((seq_len, 3 * d), lambda i: (i, 0)),
                  pl.BlockSpec((1, 1, r, d), lambda i: (layer, i, 0, 0)),
                  pl.BlockSpec((1, 1, r, d), lambda i: (layer, i, 0, 0)),
                  pl.BlockSpec((ATT_HEADS, 1, 2 * BAND_PAST), lambda i: (0, 0, 0))],
        out_specs=pl.BlockSpec((seq_len, d), lambda i: (i, 0)),
        out_shape=jax.ShapeDtypeStruct((m, d), BF16),
        compiler_params=_cparams("parallel"),
        name="att_sample",
    )(qkv, cache_k, cache_v, trows)


def _gdn_layer(xp, xs, nw, j, w_in_all, w_conv, a_log, dt_bias, w_norm, w_out_all, state_rec_all, state_conv, dec_seq):
    w_main = cast_bf16(w_in_all, j, GDN_QKV + GDN_Z)
    w_ba = jnp.pad(w_in_all[j, :, GDN_QKV + GDN_Z:], ((0, 0), (0, LANES - 2 * GDN_V_HEADS))).astype(BF16)
    w_out_b = cast_bf16(w_out_all, j)
    gate_params = jnp.zeros((2, LANES), F32)
    gate_params = gate_params.at[0, GDN_V_HEADS:2 * GDN_V_HEADS].set(a_log)
    gate_params = gate_params.at[1, GDN_V_HEADS:2 * GDN_V_HEADS].set(dt_bias)
    outs = []
    for x, seq_len, chunk, conv0, rec0, tr in (
            (xp, xp.shape[0], CHUNK, jnp.zeros((1, CONV_W - 1, GDN_QKV), F32),
             (jnp.zeros((1, 1, GDN_V_HEADS, GDN_DK, GDN_DV), F32), 0), 256),
            (xs, dec_seq, dec_seq, state_conv, (state_rec_all, j), dec_seq)):
        qkvz, ba = norm_proj(x, nw, w_main, w2=w_ba, tm=1024)
        conv8 = jnp.pad(conv0, ((0, 0), (SUBLANES - (CONV_W - 1), 0), (0, 0)))
        conv = gdn_prep(qkvz, conv8, w_conv, seq_len, tr=tr)
        gates_t, gates_r = gdn_gates(ba, gate_params, chunk)
        w, qe, kd, u, qkd, eg = gdn_chunk_prepare(conv, gates_t, gates_r, chunk)
        o, rec = gdn_chunk_scan(w, qe, kd, u, qkd, eg, qkvz, w_norm, rec0[0], rec0[1], seq_len)
        x_new = out_proj(x, o, w_out_b)
        nb = x.shape[0] // seq_len
        new_conv = qkvz.reshape(nb, seq_len, -1)[:, seq_len - (CONV_W - 1):, :GDN_QKV]
        outs.append((x_new, new_conv, rec))
    return outs


def _att_layer(xp, xs, nw, j, w_qkv_all, b_qkv, table, w_o_all, b_o, cache_k, cache_v, dec_seq):
    d = xp.shape[1]
    w_qkv_b = cast_bf16(w_qkv_all, j)
    w_o_b = cast_bf16(w_o_all, j)
    trows = _bias_rows(table)
    qkv_p = norm_proj(xp, nw, w_qkv_b, bias=b_qkv, tm=1024)
    o_p = att_prompt(qkv_p, trows)
    xp_new = out_proj(xp, o_p, w_o_b, bias=b_o, tn=d)
    lp = xp.shape[0]
    keep = min(BAND_PAST, lp)
    kp = qkv_p[lp - keep:, d:2 * d].reshape(1, keep, ATT_HEADS, ATT_DH)
    vp = qkv_p[lp - keep:, 2 * d:].reshape(1, keep, ATT_HEADS, ATT_DH)
    n_att, nb, r = cache_k.shape[:3]
    qkv_s = norm_proj(xs, nw, w_qkv_b, bias=b_qkv)
    o_s = att_sample(qkv_s, cache_k.reshape(n_att, nb, r, d), cache_v.reshape(n_att, nb, r, d), j, trows, dec_seq)
    xs_new = out_proj(xs, o_s, w_o_b, bias=b_o, tn=d)
    ks = qkv_s[:, d:2 * d].reshape(nb, dec_seq, ATT_HEADS, ATT_DH)
    vs = qkv_s[:, 2 * d:].reshape(nb, dec_seq, ATT_HEADS, ATT_DH)
    return (xp_new, kp, vp), (xs_new, ks, vs)


def kernel(x_prompt, x_sample, state_gdn_rec, state_gdn_conv, cache_att_k, cache_att_v, norm_mix, norm_ffn, norm_final, gdn_w_in, gdn_w_conv, gdn_a_log, gdn_dt_bias, gdn_w_norm, gdn_w_out, att_w_qkv, att_b_qkv, att_rel_bias, att_w_o, att_b_o, ffn_w_gate, ffn_w_up, ffn_w_down):
    bp, lp, d = x_prompt.shape
    bs, ls, _ = x_sample.shape
    assert bp == 1
    depth = norm_mix.shape[0]
    xp = x_prompt.reshape(bp * lp, d)
    xs = x_sample.reshape(bs * ls, d)
    p_rec, p_conv, p_k, p_v = [], [], [], []
    s_rec, s_conv, s_k, s_v = [], [], [], []
    for layer in range(depth):
        j = layer // 2
        if layer % 2 == 0:
            (xp, cp, rp), (xs, cs, rs) = _gdn_layer(
                xp, xs, norm_mix[layer], j, gdn_w_in, gdn_w_conv[j], gdn_a_log[j], gdn_dt_bias[j], gdn_w_norm[j],
                gdn_w_out, state_gdn_rec, state_gdn_conv[j], ls)
            p_conv.append(cp)
            p_rec.append(rp)
            s_conv.append(cs)
            s_rec.append(rs)
        else:
            (xp, kp, vp), (xs, kn, vn) = _att_layer(
                xp, xs, norm_mix[layer], j, att_w_qkv, att_b_qkv[j], att_rel_bias[j], att_w_o, att_b_o[j],
                cache_att_k, cache_att_v, ls)
            p_k.append(kp)
            p_v.append(vp)
            s_k.append(kn)
            s_v.append(vn)
        wg = cast_bf16(ffn_w_gate, layer)
        wu = cast_bf16(ffn_w_up, layer)
        wd = cast_bf16(ffn_w_down, layer)
        xp = ffn(xp, norm_ffn[layer], wg, wu, wd)
        xs = ffn(xs, norm_ffn[layer], wg, wu, wd)
    y_prompt = rmsnorm(xp, norm_final).reshape(bp, lp, d)
    y_sample = rmsnorm(xs, norm_final).reshape(bs, ls, d)
    return (y_prompt, y_sample, jnp.stack(p_rec), jnp.stack(p_conv), jnp.stack(p_k), jnp.stack(p_v),
            jnp.stack(s_rec), jnp.stack(s_conv), jnp.stack(s_k), jnp.stack(s_v))
```

```python
import functools
import math

import jax
import jax.numpy as jnp
from jax import lax
from jax.experimental import pallas as pl
from jax.experimental.pallas import tpu as pltpu

EPS = 1e-6
CHUNK = 64
GDN_QK_HEADS = 16
GDN_V_HEADS = 32
GDN_DK = 128
GDN_DV = 128
CONV_W = 4
GDN_QKV = 2 * GDN_QK_HEADS * GDN_DK + GDN_V_HEADS * GDN_DV
GDN_Z = GDN_V_HEADS * GDN_DV
ATT_HEADS = 16
ATT_DH = 128
BAND_PAST = 512
BAND = BAND_PAST + CHUNK
MAX_REL = 256
LANES = 128
SUBLANES = 8
NEG_BIG = -1e30
LOG2E = 1.4426950408889634
VMEM_LIMIT_BYTES = 56 * 1024 * 1024

BF16 = jnp.bfloat16
F32 = jnp.float32
HIGHEST = lax.Precision.HIGHEST


def _cparams(*sem):
    return pltpu.CompilerParams(dimension_semantics=sem, vmem_limit_bytes=VMEM_LIMIT_BYTES)


def _dot(a, b):
    return jnp.dot(a, b, preferred_element_type=F32)


def _dot_nt(a, b):
    return lax.dot_general(a, b, (((1,), (1,)), ((), ())), preferred_element_type=F32)


def _rms(x, w):
    return x * lax.rsqrt(jnp.mean(x * x, axis=-1, keepdims=True) + EPS) * w


def _sigmoid(x):
    return 1.0 / (1.0 + jnp.exp(-x))


def _row_tile(m, pref):
    t = min(m, pref)
    assert m % t == 0, (m, t)
    return t


def _cast_kernel(x_ref, o_ref):
    o_ref[...] = x_ref[0].astype(BF16)


def cast_bf16(w, layer, *, block_bytes=4 * 1024 * 1024):
    _, k, n = w.shape
    tk = max(SUBLANES * 2, min(k, block_bytes // (4 * n) // 16 * 16))
    while k % tk:
        tk -= 16
    return pl.pallas_call(
        _cast_kernel,
        grid=(k // tk,),
        in_specs=[pl.BlockSpec((1, tk, n), lambda i: (layer, i, 0))],
        out_specs=pl.BlockSpec((tk, n), lambda i: (i, 0)),
        out_shape=jax.ShapeDtypeStruct((k, n), BF16),
        compiler_params=_cparams("parallel"),
        name="cast_bf16",
    )(w)


def _cast_transposed_kernel(x_ref, o_ref):
    t = x_ref[0].T.astype(BF16)
    if t.shape[1] < o_ref.shape[1]:
        t = jnp.concatenate([t, jnp.zeros((t.shape[0], o_ref.shape[1] - t.shape[1]), BF16)], axis=1)
    o_ref[...] = t


def cast_bf16_transposed(wt, layer, row0, nrows, *, tr=512):
    _, _, k = wt.shape
    tr = min(tr, nrows)
    assert nrows % tr == 0 and row0 % tr == 0 and tr % SUBLANES == 0
    tc = -(-tr // LANES) * LANES
    return pl.pallas_call(
        _cast_transposed_kernel,
        grid=(nrows // tr,),
        in_specs=[pl.BlockSpec((1, tr, k), lambda i: (layer, row0 // tr + i, 0))],
        out_specs=pl.BlockSpec((k, tc), lambda i: (0, i)),
        out_shape=jax.ShapeDtypeStruct((k, nrows // tr * tc), BF16),
        compiler_params=_cparams("parallel"),
        name="cast_bf16_transposed",
    )(wt)


def _norm_proj_kernel(*refs, has_bias, has_w2):
    x_ref, nw_ref, w_ref = refs[:3]
    pos = 3
    b_ref = w2_ref = None
    if has_bias:
        b_ref = refs[pos]
        pos += 1
    if has_w2:
        w2_ref = refs[pos]
        pos += 1
    o_ref = refs[pos]
    pos += 1
    o2_ref = None
    if has_w2:
        o2_ref = refs[pos]
        pos += 1
    h_ref = refs[pos]

    @pl.when(pl.program_id(1) == 0)
    def _():
        h = _rms(x_ref[...], nw_ref[...]).astype(BF16)
        h_ref[...] = h
        if has_w2:
            o2_ref[...] = _dot(h, w2_ref[...])

    acc = _dot(h_ref[...], w_ref[...])
    if has_bias:
        acc = acc + b_ref[...]
    o_ref[...] = acc


def norm_proj(x, nw, w, bias=None, w2=None, *, tm=512, tn=1024):
    m, d = x.shape
    n = w.shape[1]
    tm = _row_tile(m, tm)
    tn = _row_tile(n, tn)
    in_specs = [pl.BlockSpec((tm, d), lambda i, j: (i, 0)),
                pl.BlockSpec((1, d), lambda i, j: (0, 0)),
                pl.BlockSpec((d, tn), lambda i, j: (0, j))]
    args = [x, nw.reshape(1, d), w]
    if bias is not None:
        in_specs.append(pl.BlockSpec((1, tn), lambda i, j: (0, j)))
        args.append(bias.reshape(1, n))
    out_shape = [jax.ShapeDtypeStruct((m, n), F32)]
    out_specs = [pl.BlockSpec((tm, tn), lambda i, j: (i, j))]
    if w2 is not None:
        n2 = w2.shape[1]
        in_specs.append(pl.BlockSpec((d, n2), lambda i, j: (0, 0)))
        args.append(w2)
        out_shape.append(jax.ShapeDtypeStruct((m, n2), F32))
        out_specs.append(pl.BlockSpec((tm, n2), lambda i, j: (i, 0)))
    res = pl.pallas_call(
        functools.partial(_norm_proj_kernel, has_bias=bias is not None, has_w2=w2 is not None),
        grid=(m // tm, n // tn),
        in_specs=in_specs, out_specs=out_specs, out_shape=out_shape,
        scratch_shapes=[pltpu.VMEM((tm, d), BF16)],
        compiler_params=_cparams("parallel", "arbitrary"),
        name="norm_proj",
    )(*args)
    return res if w2 is not None else res[0]


def _ffn_kernel(x_ref, nw_ref, wg_ref, wu_ref, wd_ref, o_ref, h_ref):
    @pl.when(pl.program_id(1) == 0)
    def _():
        x = x_ref[...]
        h_ref[...] = _rms(x, nw_ref[...]).astype(BF16)
        o_ref[...] = x

    h = h_ref[...]
    g = _dot(h, wg_ref[...])
    u = _dot(h, wu_ref[...])
    a = (g * _sigmoid(g) * u).astype(BF16)
    o_ref[...] += _dot(a, wd_ref[...])


def ffn(x, nw, wg, wu, wd, *, tm=512, tf=512):
    m, d = x.shape
    f = wg.shape[1]
    tm = _row_tile(m, tm)
    tf = _row_tile(f, tf)
    return pl.pallas_call(
        _ffn_kernel,
        grid=(m // tm, f // tf),
        in_specs=[pl.BlockSpec((tm, d), lambda i, j: (i, 0)),
                  pl.BlockSpec((1, d), lambda i, j: (0, 0)),
                  pl.BlockSpec((d, tf), lambda i, j: (0, j)),
                  pl.BlockSpec((d, tf), lambda i, j: (0, j)),
                  pl.BlockSpec((tf, d), lambda i, j: (j, 0))],
        out_specs=pl.BlockSpec((tm, d), lambda i, j: (i, 0)),
        out_shape=jax.ShapeDtypeStruct((m, d), F32),
        scratch_shapes=[pltpu.VMEM((tm, d), BF16)],
        compiler_params=_cparams("parallel", "arbitrary"),
        name="ffn",
    )(x, nw.reshape(1, d), wg, wu, wd)


def _out_proj_kernel(*refs, has_bias):
    a_ref, w_ref = refs[:2]
    b_ref = refs[2] if has_bias else None
    res_ref, o_ref = refs[-2:]
    acc = res_ref[...] + _dot(a_ref[...], w_ref[...])
    if has_bias:
        acc = acc + b_ref[...]
    o_ref[...] = acc


def out_proj(res, a, w, bias=None, *, tm=512, tn=1024):
    m, k = a.shape
    n = w.shape[1]
    assert a.dtype == BF16 and w.dtype == BF16
    tm = _row_tile(m, tm)
    tn = _row_tile(n, tn)
    in_specs = [pl.BlockSpec((tm, k), lambda i, j: (i, 0)),
                pl.BlockSpec((k, tn), lambda i, j: (0, j))]
    args = [a, w]
    if bias is not None:
        in_specs.append(pl.BlockSpec((1, tn), lambda i, j: (0, j)))
        args.append(bias.reshape(1, n))
    in_specs.append(pl.BlockSpec((tm, tn), lambda i, j: (i, j)))
    args.append(res)
    return pl.pallas_call(
        functools.partial(_out_proj_kernel, has_bias=bias is not None),
        grid=(m // tm, n // tn),
        in_specs=in_specs,
        out_specs=pl.BlockSpec((tm, tn), lambda i, j: (i, j)),
        out_shape=jax.ShapeDtypeStruct((m, n), F32),
        compiler_params=_cparams("parallel", "parallel"),
        name="out_proj",
    )(*args)


def _rmsnorm_kernel(x_ref, w_ref, o_ref):
    o_ref[...] = _rms(x_ref[...], w_ref[...])


def rmsnorm(x, w, *, tm=512):
    m, d = x.shape
    tm = _row_tile(m, tm)
    return pl.pallas_call(
        _rmsnorm_kernel,
        grid=(m // tm,),
        in_specs=[pl.BlockSpec((tm, d), lambda i: (i, 0)), pl.BlockSpec((1, d), lambda i: (0, 0))],
        out_specs=pl.BlockSpec((tm, d), lambda i: (i, 0)),
        out_shape=jax.ShapeDtypeStruct((m, d), F32),
        compiler_params=_cparams("parallel"),
        name="final_rmsnorm",
    )(x, w.reshape(1, d))


def _gdn_prep_kernel(x_ref, p_ref, st_ref, w_ref, o_ref, *, tiles_per_seq, n_q_blocks, n_qk_blocks, rb):
    i = pl.program_id(0)
    j = pl.program_id(1)
    tr, tc = x_ref.shape
    w = w_ref[...]
    first = (i % tiles_per_seq) == 0
    prev8 = jnp.where(first, st_ref[0], p_ref[...])

    def strip(halo, body, scale):
        outs = []
        for c in range(tc // LANES):
            sl = slice(c * LANES, (c + 1) * LANES)
            xx = jnp.concatenate([halo[:, sl], body[:, sl]], axis=0)
            acc = xx * w[CONV_W - 1:CONV_W, sl]
            for s in range(1, CONV_W):
                acc = acc + pltpu.roll(xx, s, axis=0) * w[CONV_W - 1 - s:CONV_W - s, sl]
            a = acc[SUBLANES:]
            a = a * _sigmoid(a)
            if scale is not None:
                a = a * (lax.rsqrt(jnp.sum(a * a, axis=-1, keepdims=True) + EPS) * scale)
            outs.append(a)
        return jnp.concatenate(outs, axis=1)

    def run(scale):
        o_ref[:rb, :] = strip(prev8, x_ref[:rb, :], scale)

        def step(b, carry):
            r0 = pl.multiple_of(b * rb, rb)
            o_ref[pl.ds(r0, rb), :] = strip(x_ref[pl.ds(r0 - SUBLANES, SUBLANES), :], x_ref[pl.ds(r0, rb), :], scale)
            return carry

        lax.fori_loop(1, tr // rb, step, 0, unroll=True)

    @pl.when(j < n_qk_blocks)
    def _():
        run(jnp.where(j < n_q_blocks, GDN_DK ** -0.5, 1.0))

    @pl.when(j >= n_qk_blocks)
    def _():
        run(None)


def gdn_prep(qkvz, conv_state8, w_conv, seq_len, *, tr, tc=1024, rb=32):
    m = qkvz.shape[0]
    assert seq_len % tr == 0 and tr % rb == 0 and rb % SUBLANES == 0
    tiles_per_seq = seq_len // tr
    nq = GDN_QK_HEADS * GDN_DK // tc
    return pl.pallas_call(
        functools.partial(_gdn_prep_kernel, tiles_per_seq=tiles_per_seq, n_q_blocks=nq, n_qk_blocks=2 * nq, rb=rb),
        grid=(m // tr, GDN_QKV // tc),
        in_specs=[pl.BlockSpec((tr, tc), lambda i, j: (i, j)),
                  pl.BlockSpec((SUBLANES, tc), lambda i, j: (jnp.maximum(i * (tr // SUBLANES) - 1, 0), j)),
                  pl.BlockSpec((1, SUBLANES, tc), lambda i, j: (i // tiles_per_seq, 0, j)),
                  pl.BlockSpec((CONV_W, tc), lambda i, j: (0, j))],
        out_specs=pl.BlockSpec((tr, tc), lambda i, j: (i, j)),
        out_shape=jax.ShapeDtypeStruct((m, GDN_QKV), F32),
        compiler_params=_cparams("parallel", "parallel"),
        name="gdn_prep",
    )(qkvz, qkvz, conv_state8, w_conv)


def _gdn_gates_kernel(ba_ref, gp_ref, t_ref, r_ref, *, chunk):
    n = ba_ref.shape[0] // chunk
    gp = gp_ref[...]
    row = lax.broadcasted_iota(jnp.int32, (chunk, chunk), 0)
    col = lax.broadcasted_iota(jnp.int32, (chunk, chunk), 1)
    tri = jnp.where(row >= col, 1.0, 0.0).astype(F32)
    lane = lax.broadcasted_iota(jnp.int32, (chunk, LANES), 1)
    for ci in range(n):
        rows = slice(ci * chunk, (ci + 1) * chunk)
        ba = ba_ref[rows, :]
        xs = ba + gp[1:2]
        softplus = jnp.maximum(xs, 0.0) + jnp.log1p(jnp.exp(-jnp.abs(xs)))
        g = -jnp.exp(gp[0:1]) * softplus
        gcum = jnp.dot(tri, g, precision=HIGHEST, preferred_element_type=F32)
        t = jnp.where(lane < GDN_V_HEADS, _sigmoid(ba), gcum)
        t_ref[rows, :] = t
        r_ref[ci] = t.T


def gdn_gates(ba, gate_params, chunk, *, chunks_per_step=8):
    m = ba.shape[0]
    nc = m // chunk
    n = min(chunks_per_step, nc)
    assert nc * chunk == m and nc % n == 0
    return pl.pallas_call(
        functools.partial(_gdn_gates_kernel, chunk=chunk),
        grid=(nc // n,),
        in_specs=[pl.BlockSpec((n * chunk, LANES), lambda i: (i, 0)),
                  pl.BlockSpec((2, LANES), lambda i: (0, 0))],
        out_specs=[pl.BlockSpec((n * chunk, LANES), lambda i: (i, 0)),
                   pl.BlockSpec((n, LANES, chunk), lambda i: (i, 0, 0))],
        out_shape=[jax.ShapeDtypeStruct((m, LANES), F32), jax.ShapeDtypeStruct((nc, LANES, chunk), F32)],
        compiler_params=_cparams("parallel"),
        name="gdn_gates",
    )(ba, gate_params)


def _gdn_prepare_kernel(q_ref, k_ref, v_ref, t_ref, r_ref, w_ref, qe_ref, kd_ref, u_ref, qkd_ref, eg_ref):
    C = q_ref.shape[0]
    nh = GDN_V_HEADS
    row = lax.broadcasted_iota(jnp.int32, (C, C), 0)
    col = lax.broadcasted_iota(jnp.int32, (C, C), 1)
    incl = (row >= col)[None]
    off_diag = (row != col)[None]
    eye = jnp.where(row == col, 1.0, 0.0).astype(F32)[None]
    n_levels = int(math.log2(C))
    assert 1 << n_levels == C
    bmm = functools.partial(jnp.einsum, "gij,gjk->gik", preferred_element_type=F32)

    def split(a):
        hi = a.astype(BF16)
        return hi, (a - hi.astype(F32)).astype(BF16)

    t_cols = t_ref[...]
    t_rows = r_ref[0]
    b_col = jnp.stack([t_cols[:, r:r + 1] for r in range(nh)])
    gc_col = jnp.stack([t_cols[:, nh + r:nh + r + 1] for r in range(nh)])
    b_row = jnp.stack([t_rows[r:r + 1, :] for r in range(nh)])
    gc_row = jnp.stack([t_rows[nh + r:nh + r + 1, :] for r in range(nh)])
    qk_heads = [(r * GDN_QK_HEADS) // GDN_V_HEADS for r in range(nh)]
    q = jnp.stack([q_ref[:, h * GDN_DK:(h + 1) * GDN_DK] for h in qk_heads])
    k = jnp.stack([k_ref[:, h * GDN_DK:(h + 1) * GDN_DK] for h in qk_heads])
    v = jnp.stack([v_ref[:, r * GDN_DV:(r + 1) * GDN_DV] for r in range(nh)])
    g_last = gc_row[:, :, C - 1:C]
    decay = jnp.exp(jnp.where(incl, gc_col - gc_row, -jnp.inf))
    kb = k.astype(BF16)
    qkk = jnp.einsum("gik,gjk->gij", jnp.concatenate([q.astype(BF16), kb], axis=1), kb,
                     preferred_element_type=F32)
    p0 = -jnp.where(off_diag, qkk[:, C:] * decay * b_col, 0.0)
    pb = p0.astype(BF16)
    x = eye + p0
    p = bmm(pb, pb)
    for lvl in range(1, n_levels):
        pb = p.astype(BF16)
        if lvl + 1 < n_levels:
            xp = bmm(pb, jnp.concatenate([x, p], axis=2).astype(BF16))
            x = x + xp[:, :, :C]
            p = xp[:, :, C:]
        else:
            x = x + bmm(pb, x.astype(BF16))
    t0 = x.astype(BF16)
    ah, al = split(eye - p0)
    at = bmm(jnp.concatenate([ah, al], axis=1), t0)
    resid = eye - (at[:, :C] + at[:, C:])
    t = t0.astype(F32) + bmm(t0, resid.astype(BF16))
    tb = t * b_row
    tbh, tbl = split(tb)
    vh, vl = split(v)
    uu = bmm(tbh, jnp.concatenate([vh, vl], axis=2))
    u_ref[...] = uu[:, :, :GDN_DV] + uu[:, :, GDN_DV:] + bmm(tbl, vh)
    w_ref[...] = bmm((tb * jnp.exp(gc_row)).astype(BF16), kb).astype(BF16)
    qkd_ref[...] = (qkk[:, :C] * decay).astype(BF16)
    qe_ref[...] = (q * jnp.exp(gc_col)).astype(BF16)
    kd_ref[...] = (k * jnp.exp(g_last - gc_col)).astype(BF16)
    eg_ref[0] = jnp.broadcast_to(jnp.exp(g_last), (nh, 1, LANES))


def gdn_chunk_prepare(conv, gates_t, gates_r, chunk):
    m = conv.shape[0]
    nc = m // chunk
    assert nc * chunk == m
    nqk = GDN_QK_HEADS * GDN_DK
    hm = lambda last: pl.BlockSpec((GDN_V_HEADS, chunk, last), lambda c: (0, c, 0))
    hshape = lambda last, dt: jax.ShapeDtypeStruct((GDN_V_HEADS, m, last), dt)
    return pl.pallas_call(
        _gdn_prepare_kernel,
        grid=(nc,),
        in_specs=[pl.BlockSpec((chunk, nqk), lambda c: (c, 0)),
                  pl.BlockSpec((chunk, nqk), lambda c: (c, 1)),
                  pl.BlockSpec((chunk, GDN_Z), lambda c: (c, 2 * nqk // GDN_Z)),
                  pl.BlockSpec((chunk, LANES), lambda c: (c, 0)),
                  pl.BlockSpec((1, LANES, chunk), lambda c: (c, 0, 0))],
        out_specs=[hm(GDN_DV), hm(GDN_DK), hm(GDN_DK), hm(GDN_DV), hm(chunk),
                   pl.BlockSpec((1, GDN_V_HEADS, 1, LANES), lambda c: (c, 0, 0, 0))],
        out_shape=[hshape(GDN_DV, BF16), hshape(GDN_DK, BF16), hshape(GDN_DK, BF16), hshape(GDN_DV, F32),
                   hshape(chunk, BF16), jax.ShapeDtypeStruct((nc, GDN_V_HEADS, 1, LANES), F32)],
        compiler_params=_cparams("parallel"),
        name="gdn_chunk_prepare",
    )(conv, conv, conv, gates_t, gates_r)


def _gdn_scan_kernel(w_ref, qe_ref, kd_ref, u_ref, qkd_ref, eg_ref, z_ref, wn_ref, s0_ref, o_ref, s_ref, *, cb):
    nh = w_ref.shape[0]
    C = qkd_ref.shape[2]

    @pl.when(pl.program_id(2) == 0)
    def _():
        s_ref[...] = s0_ref[0]

    s = s_ref[0]
    for ci in range(cb):
        rows = slice(ci * C, (ci + 1) * C)
        sb = s.astype(BF16)
        wq = jnp.concatenate([w_ref[:, rows, :], qe_ref[:, rows, :]], axis=1)
        r = jnp.einsum("hck,hkv->hcv", wq, sb, preferred_element_type=F32)
        vb = (u_ref[:, rows, :] - r[:, :C]).astype(BF16)
        o = r[:, C:] + jnp.einsum("hij,hjv->hiv", qkd_ref[:, rows, :], vb, preferred_element_type=F32)
        s = s * eg_ref[ci] + jnp.einsum("hck,hcv->hkv", kd_ref[:, rows, :], vb, preferred_element_type=F32)
        o = o * lax.rsqrt(jnp.mean(o * o, axis=-1, keepdims=True) + EPS) * wn_ref[...]
        for h in range(nh):
            z = z_ref[rows, h * GDN_DV:(h + 1) * GDN_DV]
            o_ref[rows, h * GDN_DV:(h + 1) * GDN_DV] = (o[h] * (z * _sigmoid(z))).astype(BF16)
    s_ref[0] = s


def gdn_chunk_scan(w, qe, kd, u, qkd, eg, qkvz, w_norm, s0, layer, seq_len, *, nh=32, cb=4):
    m = u.shape[1]
    chunk = qkd.shape[2]
    b = m // seq_len
    nc = seq_len // chunk
    cb = min(cb, nc)
    assert b * seq_len == m and nc % cb == 0 and GDN_V_HEADS % nh == 0
    steps = nc // cb
    z_off = GDN_QKV // (nh * GDN_DV)
    hm = lambda last: pl.BlockSpec((nh, cb * chunk, last), lambda bi, hg, c: (hg, bi * steps + c, 0))
    return pl.pallas_call(
        functools.partial(_gdn_scan_kernel, cb=cb),
        grid=(b, GDN_V_HEADS // nh, steps),
        in_specs=[hm(GDN_DV), hm(GDN_DK), hm(GDN_DK), hm(GDN_DV), hm(chunk),
                  pl.BlockSpec((cb, nh, 1, LANES), lambda bi, hg, c: (bi * steps + c, hg, 0, 0)),
                  pl.BlockSpec((cb * chunk, nh * GDN_DV), lambda bi, hg, c: (bi * steps + c, z_off + hg)),
                  pl.BlockSpec((1, GDN_DV), lambda bi, hg, c: (0, 0)),
                  pl.BlockSpec((1, 1, nh, GDN_DK, GDN_DV), lambda bi, hg, c: (layer, bi, hg, 0, 0))],
        out_specs=[pl.BlockSpec((cb * chunk, nh * GDN_DV), lambda bi, hg, c: (bi * steps + c, hg)),
                   pl.BlockSpec((1, nh, GDN_DK, GDN_DV), lambda bi, hg, c: (bi, hg, 0, 0))],
        out_shape=[jax.ShapeDtypeStruct((m, GDN_Z), BF16),
                   jax.ShapeDtypeStruct((b, GDN_V_HEADS, GDN_DK, GDN_DV), F32)],
        compiler_params=_cparams("parallel", "parallel", "arbitrary"),
        name="gdn_chunk_scan",
    )(w, qe, kd, u, qkd, eg, qkvz, w_norm.reshape(1, GDN_DV), s0)


def _toeplitz_bias(trow, rows):
    t = jnp.broadcast_to(trow, (rows, trow.shape[1]))
    return pltpu.roll(t, 0, axis=1, stride=1, stride_axis=0)


def _bias_rows(table):
    assert BAND_PAST == 2 * MAX_REL
    far = table[:, 2 * MAX_REL:]
    left = jnp.broadcast_to(far, (table.shape[0], MAX_REL))
    right = jnp.broadcast_to(far, (table.shape[0], 2 * BAND_PAST - 3 * MAX_REL - 1))
    return jnp.concatenate([left, table[:, ::-1], right], axis=1)[:, None, :]


def _att_prompt_kernel(q_ref, kp_ref, kc_ref, vp_ref, vc_ref, trow_ref, o_ref, bias_ref):
    i = pl.program_id(1)
    qb_rows = q_ref.shape[0]
    pq, pk = bias_ref.shape
    n_parts = qb_rows // pq

    @pl.when(i == 0)
    def _():
        r = lax.broadcasted_iota(jnp.int32, bias_ref.shape, 0)
        w = lax.broadcasted_iota(jnp.int32, bias_ref.shape, 1)
        start = r - r % CHUNK
        in_band = jnp.logical_and(w >= start, w < start + BAND)
        bias_ref[...] = jnp.where(in_band, _toeplitz_bias(trow_ref[0], pq)[:, :pk] * LOG2E, NEG_BIG)

    k = jnp.concatenate([kp_ref[...], kc_ref[...]], axis=0).astype(BF16)
    v = jnp.concatenate([vp_ref[...], vc_ref[...]], axis=0).astype(BF16)

    def scores(p):
        q = (q_ref[p * pq:(p + 1) * pq, :] * (ATT_DH ** -0.5 * LOG2E)).astype(BF16)
        return _dot_nt(q, k[p * pq:p * pq + pk]) + bias_ref[...]

    def body(first_block):
        col = lax.broadcasted_iota(jnp.int32, (pq, pk), 1)
        ss = [scores(p) for p in range(n_parts)]
        es, ls = [], []
        for p, s in enumerate(ss):
            if first_block:
                s = jnp.where(col < qb_rows - p * pq, NEG_BIG, s)
            e = jnp.exp2(s - jnp.max(s, axis=-1, keepdims=True))
            es.append(e.astype(BF16))
            ls.append(jnp.sum(e, axis=-1, keepdims=True))
        for p in range(n_parts):
            o_ref[p * pq:(p + 1) * pq, :] = (_dot(es[p], v[p * pq:p * pq + pk]) / ls[p]).astype(BF16)

    @pl.when(i == 0)
    def _():
        body(True)

    @pl.when(i > 0)
    def _():
        body(False)


def att_prompt(qkv, trows):
    l = qkv.shape[0]
    qb = BAND_PAST
    assert l % qb == 0
    h = ATT_HEADS
    prev = lambda i: jnp.maximum(i - 1, 0)
    return pl.pallas_call(
        _att_prompt_kernel,
        grid=(h, l // qb),
        in_specs=[pl.BlockSpec((qb, ATT_DH), lambda hh, i: (i, hh)),
                  pl.BlockSpec((qb, ATT_DH), lambda hh, i: (prev(i), h + hh)),
                  pl.BlockSpec((qb, ATT_DH), lambda hh, i: (i, h + hh)),
                  pl.BlockSpec((qb, ATT_DH), lambda hh, i: (prev(i), 2 * h + hh)),
                  pl.BlockSpec((qb, ATT_DH), lambda hh, i: (i, 2 * h + hh)),
                  pl.BlockSpec((1, 1, 2 * qb), lambda hh, i: (hh, 0, 0))],
        out_specs=pl.BlockSpec((qb, ATT_DH), lambda hh, i: (i, hh)),
        out_shape=jax.ShapeDtypeStruct((l, h * ATT_DH), BF16),
        scratch_shapes=[pltpu.VMEM((qb // 4, qb // 4 + BAND_PAST), F32)],
        compiler_params=_cparams("parallel", "arbitrary"),
        name="att_prompt",
    )(qkv, qkv, qkv, qkv, qkv, trows)


def _att_sample_kernel(qkv_ref, ck_ref, cv_ref, trow_ref, o_ref):
    d = ATT_HEADS * ATT_DH
    n_new = qkv_ref.shape[0]
    n_old = ck_ref.shape[2]
    for h in range(ATT_HEADS):
        sl = slice(h * ATT_DH, (h + 1) * ATT_DH)
        q = qkv_ref[:, sl].astype(BF16)
        kn = qkv_ref[:, d + h * ATT_DH:d + (h + 1) * ATT_DH].astype(BF16)
        vn = qkv_ref[:, 2 * d + h * ATT_DH:2 * d + (h + 1) * ATT_DH].astype(BF16)
        ck = ck_ref[0, 0, :, sl].astype(BF16)
        cv = cv_ref[0, 0, :, sl].astype(BF16)
        bias = _toeplitz_bias(trow_ref[h], n_new)
        s1 = _dot_nt(q, ck) * ATT_DH ** -0.5 + bias[:, :n_old]
        s2 = _dot_nt(q, kn) * ATT_DH ** -0.5 + bias[:, n_old:n_old + n_new]
        m = jnp.maximum(jnp.max(s1, axis=-1, keepdims=True), jnp.max(s2, axis=-1, keepdims=True))
        p1 = jnp.exp(s1 - m)
        p2 = jnp.exp(s2 - m)
        l = jnp.sum(p1, axis=-1, keepdims=True) + jnp.sum(p2, axis=-1, keepdims=True)
        o_ref[:, sl] = ((_dot(p1.astype(BF16), cv) + _dot(p2.astype(BF16), vn)) / l).astype(BF16)


def att_sample(qkv, cache_k, cache_v, layer, trows, seq_len):
    m = qkv.shape[0]
    b = m // seq_len
    r = cache_k.shape[2]
    d = ATT_HEADS * ATT_DH
    assert r == BAND_PAST and r + seq_len <= 2 * BAND_PAST
    return pl.pallas_call(
        _att_sample_kernel,
        grid=(b,),
        in_specs=[pl.BlockSpec((seq_len, 3 * d), lambda i: (i, 0)),
                  pl.BlockSpec((1, 1, r, d), lambda i: (layer, i, 0, 0)),
                  pl.BlockSpec((1, 1, r, d), lambda i: (layer, i, 0, 0)),
                  pl.BlockSpec((ATT_HEADS, 1, 2 * BAND_PAST), lambda i: (0, 0, 0))],
        out_specs=pl.BlockSpec((seq_len, d), lambda i: (i, 0)),
        out_shape=jax.ShapeDtypeStruct((m, d), BF16),
        compiler_params=_cparams("parallel"),
        name="att_sample",
    )(qkv, cache_k, cache_v, trows)


def _gdn_layer(xp, xs, nw, j, w_in_all, w_conv, a_log, dt_bias, w_norm, w_out_all, state_rec_all, state_conv, dec_seq):
    w_in_t = jnp.swapaxes(w_in_all, 1, 2)
    w_main = cast_bf16_transposed(w_in_t, j, 0, GDN_QKV + GDN_Z)
    w_ba = cast_bf16_transposed(w_in_t, j, GDN_QKV + GDN_Z, 2 * GDN_V_HEADS)
    w_out_b = cast_bf16(w_out_all, j)
    gate_params = jnp.zeros((2, LANES), F32)
    gate_params = gate_params.at[0, GDN_V_HEADS:2 * GDN_V_HEADS].set(a_log)
    gate_params = gate_params.at[1, GDN_V_HEADS:2 * GDN_V_HEADS].set(dt_bias)
    outs = []
    for x, seq_len, chunk, conv0, rec0, tr in (
            (xp, xp.shape[0], CHUNK, jnp.zeros((1, CONV_W - 1, GDN_QKV), F32),
             (jnp.zeros((1, 1, GDN_V_HEADS, GDN_DK, GDN_DV), F32), 0), 256),
            (xs, dec_seq, dec_seq, state_conv, (state_rec_all, j), dec_seq)):
        qkvz, ba = norm_proj(x, nw, w_main, w2=w_ba, tm=1024)
        conv8 = jnp.pad(conv0, ((0, 0), (SUBLANES - (CONV_W - 1), 0), (0, 0)))
        conv = gdn_prep(qkvz, conv8, w_conv, seq_len, tr=tr)
        gates_t, gates_r = gdn_gates(ba, gate_params, chunk)
        w, qe, kd, u, qkd, eg = gdn_chunk_prepare(conv, gates_t, gates_r, chunk)
        o, rec = gdn_chunk_scan(w, qe, kd, u, qkd, eg, qkvz, w_norm, rec0[0], rec0[1], seq_len)
        x_new = out_proj(x, o, w_out_b)
        nb = x.shape[0] // seq_len
        new_conv = qkvz.reshape(nb, seq_len, -1)[:, seq_len - (CONV_W - 1):, :GDN_QKV]
        outs.append((x_new, new_conv, rec))
    return outs


def _att_layer(xp, xs, nw, j, w_qkv_all, b_qkv, table, w_o_all, b_o, cache_k, cache_v, dec_seq):
    d = xp.shape[1]
    w_qkv_b = cast_bf16(w_qkv_all, j)
    w_o_b = cast_bf16(w_o_all, j)
    trows = _bias_rows(table)
    qkv_p = norm_proj(xp, nw, w_qkv_b, bias=b_qkv, tm=1024)
    o_p = att_prompt(qkv_p, trows)
    xp_new = out_proj(xp, o_p, w_o_b, bias=b_o, tn=d)
    lp = xp.shape[0]
    keep = min(BAND_PAST, lp)
    kp = qkv_p[lp - keep:, d:2 * d].reshape(1, keep, ATT_HEADS, ATT_DH)
    vp = qkv_p[lp - keep:, 2 * d:].reshape(1, keep, ATT_HEADS, ATT_DH)
    n_att, nb, r = cache_k.shape[:3]
    qkv_s = norm_proj(xs, nw, w_qkv_b, bias=b_qkv)
    o_s = att_sample(qkv_s, cache_k.reshape(n_att, nb, r, d), cache_v.reshape(n_att, nb, r, d), j, trows, dec_seq)
    xs_new = out_proj(xs, o_s, w_o_b, bias=b_o, tn=d)
    ks = qkv_s[:, d:2 * d].reshape(nb, dec_seq, ATT_HEADS, ATT_DH)
    vs = qkv_s[:, 2 * d:].reshape(nb, dec_seq, ATT_HEADS, ATT_DH)
    return (xp_new, kp, vp), (xs_new, ks, vs)


def kernel(x_prompt, x_sample, state_gdn_rec, state_gdn_conv, cache_att_k, cache_att_v, norm_mix, norm_ffn, norm_final, gdn_w_in, gdn_w_conv, gdn_a_log, gdn_dt_bias, gdn_w_norm, gdn_w_out, att_w_qkv, att_b_qkv, att_rel_bias, att_w_o, att_b_o, ffn_w_gate, ffn_w_up, ffn_w_down):
    bp, lp, d = x_prompt.shape
    bs, ls, _ = x_sample.shape
    assert bp == 1
    depth = norm_mix.shape[0]
    xp = x_prompt.reshape(bp * lp, d)
    xs = x_sample.reshape(bs * ls, d)
    p_rec, p_conv, p_k, p_v = [], [], [], []
    s_rec, s_conv, s_k, s_v = [], [], [], []
    for layer in range(depth):
        j = layer // 2
        if layer % 2 == 0:
            (xp, cp, rp), (xs, cs, rs) = _gdn_layer(
                xp, xs, norm_mix[layer], j, gdn_w_in, gdn_w_conv[j], gdn_a_log[j], gdn_dt_bias[j], gdn_w_norm[j],
                gdn_w_out, state_gdn_rec, state_gdn_conv[j], ls)
            p_conv.append(cp)
            p_rec.append(rp)
            s_conv.append(cs)
            s_rec.append(rs)
        else:
            (xp, kp, vp), (xs, kn, vn) = _att_layer(
                xp, xs, norm_mix[layer], j, att_w_qkv, att_b_qkv[j], att_rel_bias[j], att_w_o, att_b_o[j],
                cache_att_k, cache_att_v, ls)
            p_k.append(kp)
            p_v.append(vp)
            s_k.append(kn)
            s_v.append(vn)
        wg = cast_bf16(ffn_w_gate, layer)
        wu = cast_bf16(ffn_w_up, layer)
        wd = cast_bf16(ffn_w_down, layer)
        xp = ffn(xp, norm_ffn[layer], wg, wu, wd)
        xs = ffn(xs, norm_ffn[layer], wg, wu, wd)
    y_prompt = rmsnorm(xp, norm_final).reshape(bp, lp, d)
    y_sample = rmsnorm(xs, norm_final).reshape(bs, ls, d)
    return (y_prompt, y_sample, jnp.stack(p_rec), jnp.stack(p_conv), jnp.stack(p_k), jnp.stack(p_v),
            jnp.stack(s_rec), jnp.stack(s_conv), jnp.stack(s_k), jnp.stack(s_v))
```

```python
import functools
import math

import jax
import jax.numpy as jnp
from jax import lax
from jax.experimental import pallas as pl
from jax.experimental.pallas import tpu as pltpu

EPS = 1e-6
CHUNK = 64
GDN_QK_HEADS = 16
GDN_V_HEADS = 32
GDN_DK = 128
GDN_DV = 128
CONV_W = 4
GDN_QKV = 2 * GDN_QK_HEADS * GDN_DK + GDN_V_HEADS * GDN_DV
GDN_Z = GDN_V_HEADS * GDN_DV
ATT_HEADS = 16
ATT_DH = 128
BAND_PAST = 512
BAND = BAND_PAST + CHUNK
MAX_REL = 256
LANES = 128
SUBLANES = 8
NEG_BIG = -1e30
LOG2E = 1.4426950408889634
VMEM_LIMIT_BYTES = 56 * 1024 * 1024

BF16 = jnp.bfloat16
F32 = jnp.float32
HIGHEST = lax.Precision.HIGHEST


def _cparams(*sem):
    return pltpu.CompilerParams(dimension_semantics=sem, vmem_limit_bytes=VMEM_LIMIT_BYTES)


def _dot(a, b):
    return jnp.dot(a, b, preferred_element_type=F32)


def _dot_nt(a, b):
    return lax.dot_general(a, b, (((1,), (1,)), ((), ())), preferred_element_type=F32)


def _rms(x, w):
    return x * lax.rsqrt(jnp.mean(x * x, axis=-1, keepdims=True) + EPS) * w


def _sigmoid(x):
    return 1.0 / (1.0 + jnp.exp(-x))


def _row_tile(m, pref):
    t = min(m, pref)
    assert m % t == 0, (m, t)
    return t


def _cast_kernel(x_ref, o_ref):
    o_ref[...] = x_ref[0].astype(BF16)


def cast_bf16(w, layer, *, block_bytes=4 * 1024 * 1024):
    _, k, n = w.shape
    tk = max(SUBLANES * 2, min(k, block_bytes // (4 * n) // 16 * 16))
    while k % tk:
        tk -= 16
    return pl.pallas_call(
        _cast_kernel,
        grid=(k // tk,),
        in_specs=[pl.BlockSpec((1, tk, n), lambda i: (layer, i, 0))],
        out_specs=pl.BlockSpec((tk, n), lambda i: (i, 0)),
        out_shape=jax.ShapeDtypeStruct((k, n), BF16),
        compiler_params=_cparams("parallel"),
        name="cast_bf16",
    )(w)


def _cast_transposed_kernel(x_ref, o_ref):
    t = x_ref[0].T.astype(BF16)
    if t.shape[1] < o_ref.shape[1]:
        t = jnp.concatenate([t, jnp.zeros((t.shape[0], o_ref.shape[1] - t.shape[1]), BF16)], axis=1)
    o_ref[...] = t


def cast_bf16_transposed(wt, layer, row0, nrows, *, tr=512):
    _, _, k = wt.shape
    tr = min(tr, nrows)
    assert nrows % tr == 0 and row0 % tr == 0 and tr % SUBLANES == 0
    tc = -(-tr // LANES) * LANES
    return pl.pallas_call(
        _cast_transposed_kernel,
        grid=(nrows // tr,),
        in_specs=[pl.BlockSpec((1, tr, k), lambda i: (layer, row0 // tr + i, 0))],
        out_specs=pl.BlockSpec((k, tc), lambda i: (0, i)),
        out_shape=jax.ShapeDtypeStruct((k, nrows // tr * tc), BF16),
        compiler_params=_cparams("parallel"),
        name="cast_bf16_transposed",
    )(wt)


def _norm_proj_kernel(*refs, has_bias, has_w2):
    x_ref, nw_ref, w_ref = refs[:3]
    pos = 3
    b_ref = w2_ref = None
    if has_bias:
        b_ref = refs[pos]
        pos += 1
    if has_w2:
        w2_ref = refs[pos]
        pos += 1
    o_ref = refs[pos]
    pos += 1
    o2_ref = None
    if has_w2:
        o2_ref = refs[pos]
        pos += 1
    h_ref = refs[pos]

    @pl.when(pl.program_id(1) == 0)
    def _():
        h = _rms(x_ref[...], nw_ref[...]).astype(BF16)
        h_ref[...] = h
        if has_w2:
            o2_ref[...] = _dot(h, w2_ref[...])

    acc = _dot(h_ref[...], w_ref[...])
    if has_bias:
        acc = acc + b_ref[...]
    o_ref[...] = acc


def norm_proj(x, nw, w, bias=None, w2=None, *, tm=512, tn=1024):
    m, d = x.shape
    n = w.shape[1]
    tm = _row_tile(m, tm)
    tn = _row_tile(n, tn)
    in_specs = [pl.BlockSpec((tm, d), lambda i, j: (i, 0)),
                pl.BlockSpec((1, d), lambda i, j: (0, 0)),
                pl.BlockSpec((d, tn), lambda i, j: (0, j))]
    args = [x, nw.reshape(1, d), w]
    if bias is not None:
        in_specs.append(pl.BlockSpec((1, tn), lambda i, j: (0, j)))
        args.append(bias.reshape(1, n))
    out_shape = [jax.ShapeDtypeStruct((m, n), F32)]
    out_specs = [pl.BlockSpec((tm, tn), lambda i, j: (i, j))]
    if w2 is not None:
        n2 = w2.shape[1]
        in_specs.append(pl.BlockSpec((d, n2), lambda i, j: (0, 0)))
        args.append(w2)
        out_shape.append(jax.ShapeDtypeStruct((m, n2), F32))
        out_specs.append(pl.BlockSpec((tm, n2), lambda i, j: (i, 0)))
    res = pl.pallas_call(
        functools.partial(_norm_proj_kernel, has_bias=bias is not None, has_w2=w2 is not None),
        grid=(m // tm, n // tn),
        in_specs=in_specs, out_specs=out_specs, out_shape=out_shape,
        scratch_shapes=[pltpu.VMEM((tm, d), BF16)],
        compiler_params=_cparams("parallel", "arbitrary"),
        name="norm_proj",
    )(*args)
    return res if w2 is not None else res[0]


def _ffn_kernel(xp_ref, xs_ref, nw_ref, wg_ref, wu_ref, wd_ref, *rest, n_prompt_tiles):
    fw_ref = rest[0] if len(rest) == 4 else None
    op_ref, os_ref, h_ref = rest[-3:]
    j = pl.program_id(1)

    def step(x_ref, o_ref):
        @pl.when(j == 0)
        def _():
            x = x_ref[...]
            h_ref[...] = _rms(x, nw_ref[...]).astype(BF16)
            o_ref[...] = x

        h = h_ref[...]
        g = _dot(h, wg_ref[...])
        u = _dot(h, wu_ref[...])
        a = (g * _sigmoid(g) * u).astype(BF16)
        o_ref[...] += _dot(a, wd_ref[...])

        if fw_ref is not None:
            @pl.when(j == pl.num_programs(1) - 1)
            def _():
                o_ref[...] = _rms(o_ref[...], fw_ref[...])

    is_prompt = pl.program_id(0) < n_prompt_tiles

    @pl.when(is_prompt)
    def _():
        step(xp_ref, op_ref)

    @pl.when(jnp.logical_not(is_prompt))
    def _():
        step(xs_ref, os_ref)


def ffn(xp, xs, nw, wg, wu, wd, final_nw=None, *, tm=512, tf=512):
    mp, d = xp.shape
    ms = xs.shape[0]
    f = wg.shape[1]
    tm = min(tm, mp, ms)
    tf = _row_tile(f, tf)
    assert mp % tm == 0 and ms % tm == 0
    np_, ns = mp // tm, ms // tm
    p_idx = lambda i, j: (jnp.minimum(i, np_ - 1), 0)
    s_idx = lambda i, j: (jnp.maximum(i - np_, 0), 0)
    in_specs = [pl.BlockSpec((tm, d), p_idx),
                pl.BlockSpec((tm, d), s_idx),
                pl.BlockSpec((1, d), lambda i, j: (0, 0)),
                pl.BlockSpec((d, tf), lambda i, j: (0, j)),
                pl.BlockSpec((d, tf), lambda i, j: (0, j)),
                pl.BlockSpec((tf, d), lambda i, j: (j, 0))]
    args = [xp, xs, nw.reshape(1, d), wg, wu, wd]
    if final_nw is not None:
        in_specs.append(pl.BlockSpec((1, d), lambda i, j: (0, 0)))
        args.append(final_nw.reshape(1, d))
    return pl.pallas_call(
        functools.partial(_ffn_kernel, n_prompt_tiles=np_),
        grid=(np_ + ns, f // tf),
        in_specs=in_specs,
        out_specs=[pl.BlockSpec((tm, d), p_idx), pl.BlockSpec((tm, d), s_idx)],
        out_shape=[jax.ShapeDtypeStruct((mp, d), F32), jax.ShapeDtypeStruct((ms, d), F32)],
        scratch_shapes=[pltpu.VMEM((tm, d), BF16)],
        compiler_params=_cparams("arbitrary", "arbitrary"),
        name="ffn",
    )(*args)


def _out_proj_kernel(*refs, has_bias):
    a_ref, w_ref = refs[:2]
    b_ref = refs[2] if has_bias else None
    res_ref, o_ref = refs[-2:]
    acc = res_ref[...] + _dot(a_ref[...], w_ref[...])
    if has_bias:
        acc = acc + b_ref[...]
    o_ref[...] = acc


def out_proj(res, a, w, bias=None, *, tm=512, tn=1024):
    m, k = a.shape
    n = w.shape[1]
    assert a.dtype == BF16 and w.dtype == BF16
    tm = _row_tile(m, tm)
    tn = _row_tile(n, tn)
    in_specs = [pl.BlockSpec((tm, k), lambda i, j: (i, 0)),
                pl.BlockSpec((k, tn), lambda i, j: (0, j))]
    args = [a, w]
    if bias is not None:
        in_specs.append(pl.BlockSpec((1, tn), lambda i, j: (0, j)))
        args.append(bias.reshape(1, n))
    in_specs.append(pl.BlockSpec((tm, tn), lambda i, j: (i, j)))
    args.append(res)
    return pl.pallas_call(
        functools.partial(_out_proj_kernel, has_bias=bias is not None),
        grid=(m // tm, n // tn),
        in_specs=in_specs,
        out_specs=pl.BlockSpec((tm, tn), lambda i, j: (i, j)),
        out_shape=jax.ShapeDtypeStruct((m, n), F32),
        compiler_params=_cparams("parallel", "parallel"),
        name="out_proj",
    )(*args)


def _gdn_prep_kernel(x_ref, p_ref, st_ref, w_ref, o_ref, *, tiles_per_seq, n_qk_blocks, rb):
    i = pl.program_id(0)
    j = pl.program_id(1)
    tr, tc = x_ref.shape
    w = w_ref[...]
    first = (i % tiles_per_seq) == 0
    prev8 = jnp.where(first, st_ref[0], p_ref[...])

    def strip(halo, body, normalize):
        outs = []
        for c in range(tc // LANES):
            scale = jnp.where(j * (tc // LANES) + c < GDN_QK_HEADS, GDN_DK ** -0.5, 1.0)
            sl = slice(c * LANES, (c + 1) * LANES)
            xx = jnp.concatenate([halo[:, sl], body[:, sl]], axis=0)
            acc = xx * w[CONV_W - 1:CONV_W, sl]
            for s in range(1, CONV_W):
                acc = acc + pltpu.roll(xx, s, axis=0) * w[CONV_W - 1 - s:CONV_W - s, sl]
            a = acc[SUBLANES:]
            a = a * _sigmoid(a)
            if normalize:
                a = a * (lax.rsqrt(jnp.sum(a * a, axis=-1, keepdims=True) + EPS) * scale)
            outs.append(a)
        return jnp.concatenate(outs, axis=1)

    def run(normalize):
        o_ref[:rb, :] = strip(prev8, x_ref[:rb, :], normalize)

        def step(b, carry):
            r0 = pl.multiple_of(b * rb, rb)
            o_ref[pl.ds(r0, rb), :] = strip(x_ref[pl.ds(r0 - SUBLANES, SUBLANES), :], x_ref[pl.ds(r0, rb), :],
                                            normalize)
            return carry

        lax.fori_loop(1, tr // rb, step, 0, unroll=True)

    @pl.when(j < n_qk_blocks)
    def _():
        run(True)

    @pl.when(j >= n_qk_blocks)
    def _():
        run(False)


def gdn_prep(qkvz, conv_state8, w_conv, seq_len, *, tr, tc=4096, rb=32):
    m = qkvz.shape[0]
    assert seq_len % tr == 0 and tr % rb == 0 and rb % SUBLANES == 0
    tiles_per_seq = seq_len // tr
    n_qk_blocks = 2 * GDN_QK_HEADS * GDN_DK // tc
    assert n_qk_blocks * tc == 2 * GDN_QK_HEADS * GDN_DK
    return pl.pallas_call(
        functools.partial(_gdn_prep_kernel, tiles_per_seq=tiles_per_seq, n_qk_blocks=n_qk_blocks, rb=rb),
        grid=(m // tr, GDN_QKV // tc),
        in_specs=[pl.BlockSpec((tr, tc), lambda i, j: (i, j)),
                  pl.BlockSpec((SUBLANES, tc), lambda i, j: (jnp.maximum(i * (tr // SUBLANES) - 1, 0), j)),
                  pl.BlockSpec((1, SUBLANES, tc), lambda i, j: (i // tiles_per_seq, 0, j)),
                  pl.BlockSpec((CONV_W, tc), lambda i, j: (0, j))],
        out_specs=pl.BlockSpec((tr, tc), lambda i, j: (i, j)),
        out_shape=jax.ShapeDtypeStruct((m, GDN_QKV), F32),
        compiler_params=_cparams("parallel", "parallel"),
        name="gdn_prep",
    )(qkvz, qkvz, conv_state8, w_conv)


def _gdn_gates_kernel(ba_ref, gp_ref, t_ref, r_ref, *, chunk):
    n = ba_ref.shape[0] // chunk
    gp = gp_ref[...]
    row = lax.broadcasted_iota(jnp.int32, (chunk, chunk), 0)
    col = lax.broadcasted_iota(jnp.int32, (chunk, chunk), 1)
    tri = jnp.where(row >= col, 1.0, 0.0).astype(F32)
    lane = lax.broadcasted_iota(jnp.int32, (chunk, LANES), 1)
    for ci in range(n):
        rows = slice(ci * chunk, (ci + 1) * chunk)
        ba = ba_ref[rows, :]
        xs = ba + gp[1:2]
        softplus = jnp.maximum(xs, 0.0) + jnp.log1p(jnp.exp(-jnp.abs(xs)))
        g = -jnp.exp(gp[0:1]) * softplus
        gcum = jnp.dot(tri, g, precision=HIGHEST, preferred_element_type=F32)
        t = jnp.where(lane < GDN_V_HEADS, _sigmoid(ba), gcum)
        t_ref[rows, :] = t
        r_ref[ci] = t.T


def gdn_gates(ba, gate_params, chunk, *, chunks_per_step=8):
    m = ba.shape[0]
    nc = m // chunk
    n = min(chunks_per_step, nc)
    assert nc * chunk == m and nc % n == 0
    return pl.pallas_call(
        functools.partial(_gdn_gates_kernel, chunk=chunk),
        grid=(nc // n,),
        in_specs=[pl.BlockSpec((n * chunk, LANES), lambda i: (i, 0)),
                  pl.BlockSpec((2, LANES), lambda i: (0, 0))],
        out_specs=[pl.BlockSpec((n * chunk, LANES), lambda i: (i, 0)),
                   pl.BlockSpec((n, LANES, chunk), lambda i: (i, 0, 0))],
        out_shape=[jax.ShapeDtypeStruct((m, LANES), F32), jax.ShapeDtypeStruct((nc, LANES, chunk), F32)],
        compiler_params=_cparams("parallel"),
        name="gdn_gates",
    )(ba, gate_params)


def _gdn_prepare_kernel(q_ref, k_ref, v_ref, t_ref, r_ref, w_ref, qe_ref, kd_ref, u_ref, qkd_ref, eg_ref):
    C = q_ref.shape[0]
    nh = GDN_V_HEADS
    row = lax.broadcasted_iota(jnp.int32, (C, C), 0)
    col = lax.broadcasted_iota(jnp.int32, (C, C), 1)
    incl = (row >= col)[None]
    off_diag = (row != col)[None]
    eye = jnp.where(row == col, 1.0, 0.0).astype(F32)[None]
    n_levels = int(math.log2(C))
    assert 1 << n_levels == C
    bmm = functools.partial(jnp.einsum, "gij,gjk->gik", preferred_element_type=F32)

    def split(a):
        hi = a.astype(BF16)
        return hi, (a - hi.astype(F32)).astype(BF16)

    t_cols = t_ref[...]
    t_rows = r_ref[0]
    b_col = jnp.stack([t_cols[:, r:r + 1] for r in range(nh)])
    gc_col = jnp.stack([t_cols[:, nh + r:nh + r + 1] for r in range(nh)])
    b_row = jnp.stack([t_rows[r:r + 1, :] for r in range(nh)])
    gc_row = jnp.stack([t_rows[nh + r:nh + r + 1, :] for r in range(nh)])
    qk_heads = [(r * GDN_QK_HEADS) // GDN_V_HEADS for r in range(nh)]
    q = jnp.stack([q_ref[:, h * GDN_DK:(h + 1) * GDN_DK] for h in qk_heads])
    k = jnp.stack([k_ref[:, h * GDN_DK:(h + 1) * GDN_DK] for h in qk_heads])
    v = jnp.stack([v_ref[:, r * GDN_DV:(r + 1) * GDN_DV] for r in range(nh)])
    g_last = gc_row[:, :, C - 1:C]
    decay = jnp.exp(jnp.where(incl, gc_col - gc_row, -jnp.inf))
    kb = k.astype(BF16)
    qkk = jnp.einsum("gik,gjk->gij", jnp.concatenate([q.astype(BF16), kb], axis=1), kb,
                     preferred_element_type=F32)
    p0 = -jnp.where(off_diag, qkk[:, C:] * decay * b_col, 0.0)
    pb = p0.astype(BF16)
    x = eye + p0
    p = bmm(pb, pb)
    for lvl in range(1, n_levels):
        pb = p.astype(BF16)
        if lvl + 1 < n_levels:
            xp = bmm(pb, jnp.concatenate([x, p], axis=2).astype(BF16))
            x = x + xp[:, :, :C]
            p = xp[:, :, C:]
        else:
            x = x + bmm(pb, x.astype(BF16))
    t0 = x.astype(BF16)
    ah, al = split(eye - p0)
    at = bmm(jnp.concatenate([ah, al], axis=1), t0)
    resid = eye - (at[:, :C] + at[:, C:])
    t = t0.astype(F32) + bmm(t0, resid.astype(BF16))
    tb = t * b_row
    tbh, tbl = split(tb)
    vh, vl = split(v)
    uu = bmm(tbh, jnp.concatenate([vh, vl], axis=2))
    u_ref[...] = uu[:, :, :GDN_DV] + uu[:, :, GDN_DV:] + bmm(tbl, vh)
    w_ref[...] = bmm((tb * jnp.exp(gc_row)).astype(BF16), kb).astype(BF16)
    qkd_ref[...] = (qkk[:, :C] * decay).astype(BF16)
    qe_ref[...] = (q * jnp.exp(gc_col)).astype(BF16)
    kd_ref[...] = (k * jnp.exp(g_last - gc_col)).astype(BF16)
    eg_ref[0] = jnp.broadcast_to(jnp.exp(g_last), (nh, 1, LANES))


def gdn_chunk_prepare(conv, gates_t, gates_r, chunk):
    m = conv.shape[0]
    nc = m // chunk
    assert nc * chunk == m
    nqk = GDN_QK_HEADS * GDN_DK
    hm = lambda last: pl.BlockSpec((GDN_V_HEADS, chunk, last), lambda c: (0, c, 0))
    hshape = lambda last, dt: jax.ShapeDtypeStruct((GDN_V_HEADS, m, last), dt)
    return pl.pallas_call(
        _gdn_prepare_kernel,
        grid=(nc,),
        in_specs=[pl.BlockSpec((chunk, nqk), lambda c: (c, 0)),
                  pl.BlockSpec((chunk, nqk), lambda c: (c, 1)),
                  pl.BlockSpec((chunk, GDN_Z), lambda c: (c, 2 * nqk // GDN_Z)),
                  pl.BlockSpec((chunk, LANES), lambda c: (c, 0)),
                  pl.BlockSpec((1, LANES, chunk), lambda c: (c, 0, 0))],
        out_specs=[hm(GDN_DV), hm(GDN_DK), hm(GDN_DK), hm(GDN_DV), hm(chunk),
                   pl.BlockSpec((1, GDN_V_HEADS, 1, LANES), lambda c: (c, 0, 0, 0))],
        out_shape=[hshape(GDN_DV, BF16), hshape(GDN_DK, BF16), hshape(GDN_DK, BF16), hshape(GDN_DV, F32),
                   hshape(chunk, BF16), jax.ShapeDtypeStruct((nc, GDN_V_HEADS, 1, LANES), F32)],
        compiler_params=_cparams("parallel"),
        name="gdn_chunk_prepare",
    )(conv, conv, conv, gates_t, gates_r)


def _gdn_scan_kernel(w_ref, qe_ref, kd_ref, u_ref, qkd_ref, eg_ref, z_ref, wn_ref, s0_ref, o_ref, s_ref, *, cb):
    nh = w_ref.shape[0]
    C = qkd_ref.shape[2]

    @pl.when(pl.program_id(2) == 0)
    def _():
        s_ref[...] = s0_ref[0]

    s = s_ref[0]
    for ci in range(cb):
        rows = slice(ci * C, (ci + 1) * C)
        sb = s.astype(BF16)
        wq = jnp.concatenate([w_ref[:, rows, :], qe_ref[:, rows, :]], axis=1)
        r = jnp.einsum("hck,hkv->hcv", wq, sb, preferred_element_type=F32)
        vb = (u_ref[:, rows, :] - r[:, :C]).astype(BF16)
        o = r[:, C:] + jnp.einsum("hij,hjv->hiv", qkd_ref[:, rows, :], vb, preferred_element_type=F32)
        s = s * eg_ref[ci] + jnp.einsum("hck,hcv->hkv", kd_ref[:, rows, :], vb, preferred_element_type=F32)
        o = o * lax.rsqrt(jnp.mean(o * o, axis=-1, keepdims=True) + EPS) * wn_ref[...]
        for h in range(nh):
            z = z_ref[rows, h * GDN_DV:(h + 1) * GDN_DV]
            o_ref[rows, h * GDN_DV:(h + 1) * GDN_DV] = (o[h] * (z * _sigmoid(z))).astype(BF16)
    s_ref[0] = s


def gdn_chunk_scan(w, qe, kd, u, qkd, eg, qkvz, w_norm, s0, layer, seq_len, *, nh=32, cb=4):
    m = u.shape[1]
    chunk = qkd.shape[2]
    b = m // seq_len
    nc = seq_len // chunk
    cb = min(cb, nc)
    assert b * seq_len == m and nc % cb == 0 and GDN_V_HEADS % nh == 0
    steps = nc // cb
    z_off = GDN_QKV // (nh * GDN_DV)
    hm = lambda last: pl.BlockSpec((nh, cb * chunk, last), lambda bi, hg, c: (hg, bi * steps + c, 0))
    return pl.pallas_call(
        functools.partial(_gdn_scan_kernel, cb=cb),
        grid=(b, GDN_V_HEADS // nh, steps),
        in_specs=[hm(GDN_DV), hm(GDN_DK), hm(GDN_DK), hm(GDN_DV), hm(chunk),
                  pl.BlockSpec((cb, nh, 1, LANES), lambda bi, hg, c: (bi * steps + c, hg, 0, 0)),
                  pl.BlockSpec((cb * chunk, nh * GDN_DV), lambda bi, hg, c: (bi * steps + c, z_off + hg)),
                  pl.BlockSpec((1, GDN_DV), lambda bi, hg, c: (0, 0)),
                  pl.BlockSpec((1, 1, nh, GDN_DK, GDN_DV), lambda bi, hg, c: (layer, bi, hg, 0, 0))],
        out_specs=[pl.BlockSpec((cb * chunk, nh * GDN_DV), lambda bi, hg, c: (bi * steps + c, hg)),
                   pl.BlockSpec((1, nh, GDN_DK, GDN_DV), lambda bi, hg, c: (bi, hg, 0, 0))],
        out_shape=[jax.ShapeDtypeStruct((m, GDN_Z), BF16),
                   jax.ShapeDtypeStruct((b, GDN_V_HEADS, GDN_DK, GDN_DV), F32)],
        compiler_params=_cparams("parallel", "parallel", "arbitrary"),
        name="gdn_chunk_scan",
    )(w, qe, kd, u, qkd, eg, qkvz, w_norm.reshape(1, GDN_DV), s0)


def _toeplitz_bias(trow, rows):
    t = jnp.broadcast_to(trow, (rows, trow.shape[1]))
    return pltpu.roll(t, 0, axis=1, stride=1, stride_axis=0)


def _bias_rows(table):
    assert BAND_PAST == 2 * MAX_REL
    far = table[:, 2 * MAX_REL:]
    left = jnp.broadcast_to(far, (table.shape[0], MAX_REL))
    right = jnp.broadcast_to(far, (table.shape[0], 2 * BAND_PAST - 3 * MAX_REL - 1))
    return jnp.concatenate([left, table[:, ::-1], right], axis=1)[:, None, :]


def _att_prompt_kernel(q_ref, kp_ref, kc_ref, vp_ref, vc_ref, trow_ref, o_ref, bias_ref):
    i = pl.program_id(1)
    qb_rows = q_ref.shape[0]
    n_heads, pq, pk = bias_ref.shape
    n_parts = qb_rows // pq

    @pl.when(i == 0)
    def _():
        r = lax.broadcasted_iota(jnp.int32, (pq, pk), 0)
        w = lax.broadcasted_iota(jnp.int32, (pq, pk), 1)
        start = r - r % CHUNK
        in_band = jnp.logical_and(w >= start, w < start + BAND)
        for hd in range(n_heads):
            bias_ref[hd] = jnp.where(in_band, _toeplitz_bias(trow_ref[hd], pq)[:, :pk] * LOG2E, NEG_BIG)

    def body(first_block):
        col = lax.broadcasted_iota(jnp.int32, (pq, pk), 1)
        units = [(hd, p) for hd in range(n_heads) for p in range(n_parts)]
        ks, vs, ss, es, ls = [], [], [], [], []
        for hd in range(n_heads):
            hs = slice(hd * ATT_DH, (hd + 1) * ATT_DH)
            ks.append(jnp.concatenate([kp_ref[:, hs], kc_ref[:, hs]], axis=0).astype(BF16))
            vs.append(jnp.concatenate([vp_ref[:, hs], vc_ref[:, hs]], axis=0).astype(BF16))
        for hd, p in units:
            q = (q_ref[p * pq:(p + 1) * pq, hd * ATT_DH:(hd + 1) * ATT_DH] * (ATT_DH ** -0.5 * LOG2E)).astype(BF16)
            ss.append(_dot_nt(q, ks[hd][p * pq:p * pq + pk]) + bias_ref[hd])
        for (hd, p), s in zip(units, ss):
            if first_block:
                s = jnp.where(col < qb_rows - p * pq, NEG_BIG, s)
            e = jnp.exp2(s - jnp.max(s, axis=-1, keepdims=True))
            es.append(e.astype(BF16))
            ls.append(jnp.sum(e, axis=-1, keepdims=True))
        for (hd, p), e, l in zip(units, es, ls):
            o_ref[p * pq:(p + 1) * pq, hd * ATT_DH:(hd + 1) * ATT_DH] = (
                _dot(e, vs[hd][p * pq:p * pq + pk]) / l).astype(BF16)

    @pl.when(i == 0)
    def _():
        body(True)

    @pl.when(i > 0)
    def _():
        body(False)


def att_prompt(qkv, trows, *, heads_per_step=4):
    l = qkv.shape[0]
    qb = BAND_PAST
    assert l % qb == 0 and ATT_HEADS % heads_per_step == 0
    g = ATT_HEADS // heads_per_step
    wd = heads_per_step * ATT_DH
    prev = lambda i: jnp.maximum(i - 1, 0)
    return pl.pallas_call(
        _att_prompt_kernel,
        grid=(g, l // qb),
        in_specs=[pl.BlockSpec((qb, wd), lambda hh, i: (i, hh)),
                  pl.BlockSpec((qb, wd), lambda hh, i: (prev(i), g + hh)),
                  pl.BlockSpec((qb, wd), lambda hh, i: (i, g + hh)),
                  pl.BlockSpec((qb, wd), lambda hh, i: (prev(i), 2 * g + hh)),
                  pl.BlockSpec((qb, wd), lambda hh, i: (i, 2 * g + hh)),
                  pl.BlockSpec((heads_per_step, 1, 2 * qb), lambda hh, i: (hh, 0, 0))],
        out_specs=pl.BlockSpec((qb, wd), lambda hh, i: (i, hh)),
        out_shape=jax.ShapeDtypeStruct((l, ATT_HEADS * ATT_DH), BF16),
        scratch_shapes=[pltpu.VMEM((heads_per_step, qb // 4, qb // 4 + BAND_PAST), F32)],
        compiler_params=_cparams("parallel", "arbitrary"),
        name="att_prompt",
    )(qkv, qkv, qkv, qkv, qkv, trows)


def _att_sample_kernel(qkv_ref, ck_ref, cv_ref, trow_ref, o_ref):
    d = ATT_HEADS * ATT_DH
    n_new = qkv_ref.shape[0]
    n_old = ck_ref.shape[2]
    heads = range(ATT_HEADS)
    cols = lambda h, section: slice(section * d + h * ATT_DH, section * d + (h + 1) * ATT_DH)
    s1, s2, e1, e2, ls = [], [], [], [], []
    for h in heads:
        q = (qkv_ref[:, cols(h, 0)] * (ATT_DH ** -0.5 * LOG2E)).astype(BF16)
        bias = _toeplitz_bias(trow_ref[h], n_new) * LOG2E
        s1.append(_dot_nt(q, ck_ref[0, 0, :, cols(h, 0)].astype(BF16)) + bias[:, :n_old])
        s2.append(_dot_nt(q, qkv_ref[:, cols(h, 1)].astype(BF16)) + bias[:, n_old:n_old + n_new])
    for h in heads:
        m = jnp.maximum(jnp.max(s1[h], axis=-1, keepdims=True), jnp.max(s2[h], axis=-1, keepdims=True))
        a1 = jnp.exp2(s1[h] - m)
        a2 = jnp.exp2(s2[h] - m)
        ls.append(jnp.sum(a1, axis=-1, keepdims=True) + jnp.sum(a2, axis=-1, keepdims=True))
        e1.append(a1.astype(BF16))
        e2.append(a2.astype(BF16))
    for h in heads:
        o = _dot(e1[h], cv_ref[0, 0, :, cols(h, 0)].astype(BF16)) + _dot(e2[h], qkv_ref[:, cols(h, 2)].astype(BF16))
        o_ref[:, cols(h, 0)] = (o / ls[h]).astype(BF16)


def att_sample(qkv, cache_k, cache_v, layer, trows, seq_len):
    m = qkv.shape[0]
    b = m // seq_len
    r = cache_k.shape[2]
    d = ATT_HEADS * ATT_DH
    assert r == BAND_PAST and r + seq_len <= 2 * BAND_PAST
    return pl.pallas_call(
        _att_sample_kernel,
        grid=(b,),
        in_specs=[pl.BlockSpec((seq_len, 3 * d), lambda i: (i, 0)),
                  pl.BlockSpec((1, 1, r, d), lambda i: (layer, i, 0, 0)),
                  pl.BlockSpec((1, 1, r, d), lambda i: (layer, i, 0, 0)),
                  pl.BlockSpec((ATT_HEADS, 1, 2 * BAND_PAST), lambda i: (0, 0, 0))],
        out_specs=pl.BlockSpec((seq_len, d), lambda i: (i, 0)),
        out_shape=jax.ShapeDtypeStruct((m, d), BF16),
        compiler_params=_cparams("parallel"),
        name="att_sample",
    )(qkv, cache_k, cache_v, trows)


def _gdn_layer(xp, xs, nw, j, w_in_all, w_conv, a_log, dt_bias, w_norm, w_out_all, state_rec_all, state_conv, dec_seq):
    w_in_t = jnp.swapaxes(w_in_all, 1, 2)
    w_main = cast_bf16_transposed(w_in_t, j, 0, GDN_QKV + GDN_Z)
    w_ba = cast_bf16_transposed(w_in_t, j, GDN_QKV + GDN_Z, 2 * GDN_V_HEADS)
    w_out_b = cast_bf16(w_out_all, j)
    gate_params = jnp.zeros((2, LANES), F32)
    gate_params = gate_params.at[0, GDN_V_HEADS:2 * GDN_V_HEADS].set(a_log)
    gate_params = gate_params.at[1, GDN_V_HEADS:2 * GDN_V_HEADS].set(dt_bias)
    outs = []
    for x, seq_len, chunk, conv0, rec0, tr in (
            (xp, xp.shape[0], CHUNK, jnp.zeros((1, CONV_W - 1, GDN_QKV), F32),
             (jnp.zeros((1, 1, GDN_V_HEADS, GDN_DK, GDN_DV), F32), 0), 256),
            (xs, dec_seq, dec_seq, state_conv, (state_rec_all, j), dec_seq)):
        qkvz, ba = norm_proj(x, nw, w_main, w2=w_ba, tm=1024)
        conv8 = jnp.pad(conv0, ((0, 0), (SUBLANES - (CONV_W - 1), 0), (0, 0)))
        conv = gdn_prep(qkvz, conv8, w_conv, seq_len, tr=tr)
        gates_t, gates_r = gdn_gates(ba, gate_params, chunk)
        w, qe, kd, u, qkd, eg = gdn_chunk_prepare(conv, gates_t, gates_r, chunk)
        o, rec = gdn_chunk_scan(w, qe, kd, u, qkd, eg, qkvz, w_norm, rec0[0], rec0[1], seq_len)
        x_new = out_proj(x, o, w_out_b, tm=1024)
        nb = x.shape[0] // seq_len
        new_conv = qkvz.reshape(nb, seq_len, -1)[:, seq_len - (CONV_W - 1):, :GDN_QKV]
        outs.append((x_new, new_conv, rec))
    return outs


def _att_layer(xp, xs, nw, j, w_qkv_all, b_qkv, table, w_o_all, b_o, cache_k, cache_v, dec_seq):
    d = xp.shape[1]
    w_qkv_b = cast_bf16(w_qkv_all, j)
    w_o_b = cast_bf16(w_o_all, j)
    trows = _bias_rows(table)
    qkv_p = norm_proj(xp, nw, w_qkv_b, bias=b_qkv, tm=1024)
    o_p = att_prompt(qkv_p, trows)
    xp_new = out_proj(xp, o_p, w_o_b, bias=b_o, tn=d)
    lp = xp.shape[0]
    keep = min(BAND_PAST, lp)
    kp = qkv_p[lp - keep:, d:2 * d].reshape(1, keep, ATT_HEADS, ATT_DH)
    vp = qkv_p[lp - keep:, 2 * d:].reshape(1, keep, ATT_HEADS, ATT_DH)
    n_att, nb, r = cache_k.shape[:3]
    qkv_s = norm_proj(xs, nw, w_qkv_b, bias=b_qkv)
    o_s = att_sample(qkv_s, cache_k.reshape(n_att, nb, r, d), cache_v.reshape(n_att, nb, r, d), j, trows, dec_seq)
    xs_new = out_proj(xs, o_s, w_o_b, bias=b_o, tn=d)
    ks = qkv_s[:, d:2 * d].reshape(nb, dec_seq, ATT_HEADS, ATT_DH)
    vs = qkv_s[:, 2 * d:].reshape(nb, dec_seq, ATT_HEADS, ATT_DH)
    return (xp_new, kp, vp), (xs_new, ks, vs)


def kernel(x_prompt, x_sample, state_gdn_rec, state_gdn_conv, cache_att_k, cache_att_v, norm_mix, norm_ffn, norm_final, gdn_w_in, gdn_w_conv, gdn_a_log, gdn_dt_bias, gdn_w_norm, gdn_w_out, att_w_qkv, att_b_qkv, att_rel_bias, att_w_o, att_b_o, ffn_w_gate, ffn_w_up, ffn_w_down):
    bp, lp, d = x_prompt.shape
    bs, ls, _ = x_sample.shape
    assert bp == 1
    depth = norm_mix.shape[0]
    xp = x_prompt.reshape(bp * lp, d)
    xs = x_sample.reshape(bs * ls, d)
    p_rec, p_conv, p_k, p_v = [], [], [], []
    s_rec, s_conv, s_k, s_v = [], [], [], []
    for layer in range(depth):
        j = layer // 2
        if layer % 2 == 0:
            (xp, cp, rp), (xs, cs, rs) = _gdn_layer(
                xp, xs, norm_mix[layer], j, gdn_w_in, gdn_w_conv[j], gdn_a_log[j], gdn_dt_bias[j], gdn_w_norm[j],
                gdn_w_out, state_gdn_rec, state_gdn_conv[j], ls)
            p_conv.append(cp)
            p_rec.append(rp)
            s_conv.append(cs)
            s_rec.append(rs)
        else:
            (xp, kp, vp), (xs, kn, vn) = _att_layer(
                xp, xs, norm_mix[layer], j, att_w_qkv, att_b_qkv[j], att_rel_bias[j], att_w_o, att_b_o[j],
                cache_att_k, cache_att_v, ls)
            p_k.append(kp)
            p_v.append(vp)
            s_k.append(kn)
            s_v.append(vn)
        wg = cast_bf16(ffn_w_gate, layer)
        wu = cast_bf16(ffn_w_up, layer)
        wd = cast_bf16(ffn_w_down, layer)
        final_nw = norm_final if layer == depth - 1 else None
        xp, xs = ffn(xp, xs, norm_ffn[layer], wg, wu, wd, final_nw)
    y_prompt = xp.reshape(bp, lp, d)
    y_sample = xs.reshape(bs, ls, d)
    return (y_prompt, y_sample, jnp.stack(p_rec), jnp.stack(p_conv), jnp.stack(p_k), jnp.stack(p_v),
            jnp.stack(s_rec), jnp.stack(s_conv), jnp.stack(s_k), jnp.stack(s_v))
```

```python
import functools
import math

import jax
import jax.numpy as jnp
from jax import lax
from jax.experimental import pallas as pl
from jax.experimental.pallas import tpu as pltpu

EPS = 1e-6
CHUNK = 64
GDN_QK_HEADS = 16
GDN_V_HEADS = 32
GDN_DK = 128
GDN_DV = 128
CONV_W = 4
GDN_QKV = 2 * GDN_QK_HEADS * GDN_DK + GDN_V_HEADS * GDN_DV
GDN_Z = GDN_V_HEADS * GDN_DV
ATT_HEADS = 16
ATT_DH = 128
BAND_PAST = 512
BAND = BAND_PAST + CHUNK
MAX_REL = 256
LANES = 128
SUBLANES = 8
NEG_BIG = -1e30
LOG2E = 1.4426950408889634
VMEM_LIMIT_BYTES = 56 * 1024 * 1024

BF16 = jnp.bfloat16
F32 = jnp.float32
HIGHEST = lax.Precision.HIGHEST


def _cparams(*sem):
    return pltpu.CompilerParams(dimension_semantics=sem, vmem_limit_bytes=VMEM_LIMIT_BYTES)


def _dot(a, b):
    return jnp.dot(a, b, preferred_element_type=F32)


def _dot_nt(a, b):
    return lax.dot_general(a, b, (((1,), (1,)), ((), ())), preferred_element_type=F32)


def _rms(x, w):
    return x * lax.rsqrt(jnp.mean(x * x, axis=-1, keepdims=True) + EPS) * w


def _sigmoid(x):
    return 1.0 / (1.0 + jnp.exp(-x))


def _row_tile(m, pref):
    t = min(m, pref)
    assert m % t == 0, (m, t)
    return t


def _cast_kernel(x_ref, o_ref):
    o_ref[...] = x_ref[0].astype(BF16)


def cast_bf16(w, layer, *, block_bytes=4 * 1024 * 1024):
    _, k, n = w.shape
    tk = max(SUBLANES * 2, min(k, block_bytes // (4 * n) // 16 * 16))
    while k % tk:
        tk -= 16
    return pl.pallas_call(
        _cast_kernel,
        grid=(k // tk,),
        in_specs=[pl.BlockSpec((1, tk, n), lambda i: (layer, i, 0))],
        out_specs=pl.BlockSpec((tk, n), lambda i: (i, 0)),
        out_shape=jax.ShapeDtypeStruct((k, n), BF16),
        compiler_params=_cparams("parallel"),
        name="cast_bf16",
    )(w)


def _cast_transposed_kernel(x_ref, o_ref):
    t = x_ref[0].T.astype(BF16)
    if t.shape[1] < o_ref.shape[1]:
        t = jnp.concatenate([t, jnp.zeros((t.shape[0], o_ref.shape[1] - t.shape[1]), BF16)], axis=1)
    o_ref[...] = t


def cast_bf16_transposed(wt, layer, row0, nrows, *, tr=512):
    _, _, k = wt.shape
    tr = min(tr, nrows)
    assert nrows % tr == 0 and row0 % tr == 0 and tr % SUBLANES == 0
    tc = -(-tr // LANES) * LANES
    return pl.pallas_call(
        _cast_transposed_kernel,
        grid=(nrows // tr,),
        in_specs=[pl.BlockSpec((1, tr, k), lambda i: (layer, row0 // tr + i, 0))],
        out_specs=pl.BlockSpec((k, tc), lambda i: (0, i)),
        out_shape=jax.ShapeDtypeStruct((k, nrows // tr * tc), BF16),
        compiler_params=_cparams("parallel"),
        name="cast_bf16_transposed",
    )(wt)


def _norm_proj_kernel(*refs, has_bias, has_w2):
    x_ref, nw_ref, w_ref = refs[:3]
    pos = 3
    b_ref = w2_ref = None
    if has_bias:
        b_ref = refs[pos]
        pos += 1
    if has_w2:
        w2_ref = refs[pos]
        pos += 1
    o_ref = refs[pos]
    pos += 1
    o2_ref = None
    if has_w2:
        o2_ref = refs[pos]
        pos += 1
    h_ref = refs[pos]

    @pl.when(pl.program_id(1) == 0)
    def _():
        h = _rms(x_ref[...], nw_ref[...]).astype(BF16)
        h_ref[...] = h
        if has_w2:
            o2_ref[...] = _dot(h, w2_ref[...])

    acc = _dot(h_ref[...], w_ref[...])
    if has_bias:
        acc = acc + b_ref[...]
    o_ref[...] = acc


def norm_proj(x, nw, w, bias=None, w2=None, *, tm=512, tn=1024):
    m, d = x.shape
    n = w.shape[1]
    tm = _row_tile(m, tm)
    tn = _row_tile(n, tn)
    in_specs = [pl.BlockSpec((tm, d), lambda i, j: (i, 0)),
                pl.BlockSpec((1, d), lambda i, j: (0, 0)),
                pl.BlockSpec((d, tn), lambda i, j: (0, j))]
    args = [x, nw.reshape(1, d), w]
    if bias is not None:
        in_specs.append(pl.BlockSpec((1, tn), lambda i, j: (0, j)))
        args.append(bias.reshape(1, n))
    out_shape = [jax.ShapeDtypeStruct((m, n), F32)]
    out_specs = [pl.BlockSpec((tm, tn), lambda i, j: (i, j))]
    if w2 is not None:
        n2 = w2.shape[1]
        in_specs.append(pl.BlockSpec((d, n2), lambda i, j: (0, 0)))
        args.append(w2)
        out_shape.append(jax.ShapeDtypeStruct((m, n2), F32))
        out_specs.append(pl.BlockSpec((tm, n2), lambda i, j: (i, 0)))
    res = pl.pallas_call(
        functools.partial(_norm_proj_kernel, has_bias=bias is not None, has_w2=w2 is not None),
        grid=(m // tm, n // tn),
        in_specs=in_specs, out_specs=out_specs, out_shape=out_shape,
        scratch_shapes=[pltpu.VMEM((tm, d), BF16)],
        compiler_params=_cparams("parallel", "arbitrary"),
        name="norm_proj",
    )(*args)
    return res if w2 is not None else res[0]


def _ffn_kernel(xp_ref, xs_ref, nw_ref, wg_ref, wu_ref, wd_ref, *rest, n_prompt_tiles):
    fw_ref = rest[0] if len(rest) == 4 else None
    op_ref, os_ref, h_ref = rest[-3:]
    j = pl.program_id(1)

    def step(x_ref, o_ref):
        @pl.when(j == 0)
        def _():
            x = x_ref[...]
            h_ref[...] = _rms(x, nw_ref[...]).astype(BF16)
            o_ref[...] = x

        h = h_ref[...]
        g = _dot(h, wg_ref[...])
        u = _dot(h, wu_ref[...])
        a = (g * _sigmoid(g) * u).astype(BF16)
        o_ref[...] += _dot(a, wd_ref[...])

        if fw_ref is not None:
            @pl.when(j == pl.num_programs(1) - 1)
            def _():
                o_ref[...] = _rms(o_ref[...], fw_ref[...])

    is_prompt = pl.program_id(0) < n_prompt_tiles

    @pl.when(is_prompt)
    def _():
        step(xp_ref, op_ref)

    @pl.when(jnp.logical_not(is_prompt))
    def _():
        step(xs_ref, os_ref)


def ffn(xp, xs, nw, wg, wu, wd, final_nw=None, *, tm=512, tf=512):
    mp, d = xp.shape
    ms = xs.shape[0]
    f = wg.shape[1]
    tm = min(tm, mp, ms)
    tf = _row_tile(f, tf)
    assert mp % tm == 0 and ms % tm == 0
    np_, ns = mp // tm, ms // tm
    p_idx = lambda i, j: (jnp.minimum(i, np_ - 1), 0)
    s_idx = lambda i, j: (jnp.maximum(i - np_, 0), 0)
    in_specs = [pl.BlockSpec((tm, d), p_idx),
                pl.BlockSpec((tm, d), s_idx),
                pl.BlockSpec((1, d), lambda i, j: (0, 0)),
                pl.BlockSpec((d, tf), lambda i, j: (0, j)),
                pl.BlockSpec((d, tf), lambda i, j: (0, j)),
                pl.BlockSpec((tf, d), lambda i, j: (j, 0))]
    args = [xp, xs, nw.reshape(1, d), wg, wu, wd]
    if final_nw is not None:
        in_specs.append(pl.BlockSpec((1, d), lambda i, j: (0, 0)))
        args.append(final_nw.reshape(1, d))
    return pl.pallas_call(
        functools.partial(_ffn_kernel, n_prompt_tiles=np_),
        grid=(np_ + ns, f // tf),
        in_specs=in_specs,
        out_specs=[pl.BlockSpec((tm, d), p_idx), pl.BlockSpec((tm, d), s_idx)],
        out_shape=[jax.ShapeDtypeStruct((mp, d), F32), jax.ShapeDtypeStruct((ms, d), F32)],
        scratch_shapes=[pltpu.VMEM((tm, d), BF16)],
        compiler_params=_cparams("arbitrary", "arbitrary"),
        name="ffn",
    )(*args)


def _out_proj_kernel(*refs, has_bias):
    a_ref, w_ref = refs[:2]
    b_ref = refs[2] if has_bias else None
    res_ref, o_ref = refs[-2:]
    acc = res_ref[...] + _dot(a_ref[...], w_ref[...])
    if has_bias:
        acc = acc + b_ref[...]
    o_ref[...] = acc


def out_proj(res, a, w, bias=None, *, tm=512, tn=1024):
    m, k = a.shape
    n = w.shape[1]
    assert a.dtype == BF16 and w.dtype == BF16
    tm = _row_tile(m, tm)
    tn = _row_tile(n, tn)
    in_specs = [pl.BlockSpec((tm, k), lambda i, j: (i, 0)),
                pl.BlockSpec((k, tn), lambda i, j: (0, j))]
    args = [a, w]
    if bias is not None:
        in_specs.append(pl.BlockSpec((1, tn), lambda i, j: (0, j)))
        args.append(bias.reshape(1, n))
    in_specs.append(pl.BlockSpec((tm, tn), lambda i, j: (i, j)))
    args.append(res)
    return pl.pallas_call(
        functools.partial(_out_proj_kernel, has_bias=bias is not None),
        grid=(m // tm, n // tn),
        in_specs=in_specs,
        out_specs=pl.BlockSpec((tm, tn), lambda i, j: (i, j)),
        out_shape=jax.ShapeDtypeStruct((m, n), F32),
        compiler_params=_cparams("parallel", "parallel"),
        name="out_proj",
    )(*args)


def _gdn_prep_kernel(x_ref, p_ref, st_ref, w_ref, o_ref, *, tiles_per_seq, n_qk_blocks, rb):
    i = pl.program_id(0)
    j = pl.program_id(1)
    tr, tc = x_ref.shape
    w = w_ref[...]
    first = (i % tiles_per_seq) == 0
    prev8 = jnp.where(first, st_ref[0], p_ref[...])

    def strip(halo, body, normalize):
        outs = []
        for c in range(tc // LANES):
            scale = jnp.where(j * (tc // LANES) + c < GDN_QK_HEADS, GDN_DK ** -0.5, 1.0)
            sl = slice(c * LANES, (c + 1) * LANES)
            xx = jnp.concatenate([halo[:, sl], body[:, sl]], axis=0)
            acc = xx * w[CONV_W - 1:CONV_W, sl]
            for s in range(1, CONV_W):
                acc = acc + pltpu.roll(xx, s, axis=0) * w[CONV_W - 1 - s:CONV_W - s, sl]
            a = acc[SUBLANES:]
            a = a * _sigmoid(a)
            if normalize:
                a = a * (lax.rsqrt(jnp.sum(a * a, axis=-1, keepdims=True) + EPS) * scale)
            outs.append(a)
        return jnp.concatenate(outs, axis=1)

    def run(normalize):
        o_ref[:rb, :] = strip(prev8, x_ref[:rb, :], normalize)

        def step(b, carry):
            r0 = pl.multiple_of(b * rb, rb)
            o_ref[pl.ds(r0, rb), :] = strip(x_ref[pl.ds(r0 - SUBLANES, SUBLANES), :], x_ref[pl.ds(r0, rb), :],
                                            normalize)
            return carry

        lax.fori_loop(1, tr // rb, step, 0, unroll=True)

    @pl.when(j < n_qk_blocks)
    def _():
        run(True)

    @pl.when(j >= n_qk_blocks)
    def _():
        run(False)


def gdn_prep(qkvz, conv_state8, w_conv, seq_len, *, tr, tc=4096, rb=32):
    m = qkvz.shape[0]
    assert seq_len % tr == 0 and tr % rb == 0 and rb % SUBLANES == 0
    tiles_per_seq = seq_len // tr
    n_qk_blocks = 2 * GDN_QK_HEADS * GDN_DK // tc
    assert n_qk_blocks * tc == 2 * GDN_QK_HEADS * GDN_DK
    return pl.pallas_call(
        functools.partial(_gdn_prep_kernel, tiles_per_seq=tiles_per_seq, n_qk_blocks=n_qk_blocks, rb=rb),
        grid=(m // tr, GDN_QKV // tc),
        in_specs=[pl.BlockSpec((tr, tc), lambda i, j: (i, j)),
                  pl.BlockSpec((SUBLANES, tc), lambda i, j: (jnp.maximum(i * (tr // SUBLANES) - 1, 0), j)),
                  pl.BlockSpec((1, SUBLANES, tc), lambda i, j: (i // tiles_per_seq, 0, j)),
                  pl.BlockSpec((CONV_W, tc), lambda i, j: (0, j))],
        out_specs=pl.BlockSpec((tr, tc), lambda i, j: (i, j)),
        out_shape=jax.ShapeDtypeStruct((m, GDN_QKV), F32),
        compiler_params=_cparams("parallel", "parallel"),
        name="gdn_prep",
    )(qkvz, qkvz, conv_state8, w_conv)


def _gdn_gates_kernel(ba_ref, gp_ref, t_ref, r_ref, *, chunk):
    n = ba_ref.shape[0] // chunk
    gp = gp_ref[...]
    row = lax.broadcasted_iota(jnp.int32, (chunk, chunk), 0)
    col = lax.broadcasted_iota(jnp.int32, (chunk, chunk), 1)
    tri = jnp.where(row >= col, 1.0, 0.0).astype(F32)
    lane = lax.broadcasted_iota(jnp.int32, (chunk, LANES), 1)
    for ci in range(n):
        rows = slice(ci * chunk, (ci + 1) * chunk)
        ba = ba_ref[rows, :]
        xs = ba + gp[1:2]
        softplus = jnp.maximum(xs, 0.0) + jnp.log1p(jnp.exp(-jnp.abs(xs)))
        g = -jnp.exp(gp[0:1]) * softplus
        gcum = jnp.dot(tri, g, precision=HIGHEST, preferred_element_type=F32)
        t = jnp.where(lane < GDN_V_HEADS, _sigmoid(ba), gcum)
        t_ref[rows, :] = t
        r_ref[ci] = t.T


def gdn_gates(ba, gate_params, chunk, *, chunks_per_step=8):
    m = ba.shape[0]
    nc = m // chunk
    n = min(chunks_per_step, nc)
    assert nc * chunk == m and nc % n == 0
    return pl.pallas_call(
        functools.partial(_gdn_gates_kernel, chunk=chunk),
        grid=(nc // n,),
        in_specs=[pl.BlockSpec((n * chunk, LANES), lambda i: (i, 0)),
                  pl.BlockSpec((2, LANES), lambda i: (0, 0))],
        out_specs=[pl.BlockSpec((n * chunk, LANES), lambda i: (i, 0)),
                   pl.BlockSpec((n, LANES, chunk), lambda i: (i, 0, 0))],
        out_shape=[jax.ShapeDtypeStruct((m, LANES), F32), jax.ShapeDtypeStruct((nc, LANES, chunk), F32)],
        compiler_params=_cparams("parallel"),
        name="gdn_gates",
    )(ba, gate_params)


def _gdn_rule_kernel(q_ref, k_ref, v_ref, t_ref, r_ref, z_ref, wn_ref, s0_ref, o_ref, s_ref):
    C = q_ref.shape[0]
    nh = GDN_V_HEADS
    row = lax.broadcasted_iota(jnp.int32, (C, C), 0)
    col = lax.broadcasted_iota(jnp.int32, (C, C), 1)
    incl = (row >= col)[None]
    off_diag = (row != col)[None]
    eye = jnp.where(row == col, 1.0, 0.0).astype(F32)[None]
    n_levels = int(math.log2(C))
    assert 1 << n_levels == C
    bmm = functools.partial(jnp.einsum, "gij,gjk->gik", preferred_element_type=F32)

    def split(a):
        hi = a.astype(BF16)
        return hi, (a - hi.astype(F32)).astype(BF16)

    @pl.when(pl.program_id(1) == 0)
    def _():
        s_ref[...] = s0_ref[0]

    t_cols = t_ref[...]
    t_rows = r_ref[0]
    b_col = jnp.stack([t_cols[:, r:r + 1] for r in range(nh)])
    gc_col = jnp.stack([t_cols[:, nh + r:nh + r + 1] for r in range(nh)])
    b_row = jnp.stack([t_rows[r:r + 1, :] for r in range(nh)])
    gc_row = jnp.stack([t_rows[nh + r:nh + r + 1, :] for r in range(nh)])
    qk_heads = [(r * GDN_QK_HEADS) // GDN_V_HEADS for r in range(nh)]
    q = jnp.stack([q_ref[:, h * GDN_DK:(h + 1) * GDN_DK] for h in qk_heads])
    k = jnp.stack([k_ref[:, h * GDN_DK:(h + 1) * GDN_DK] for h in qk_heads])
    v = jnp.stack([v_ref[:, r * GDN_DV:(r + 1) * GDN_DV] for r in range(nh)])
    g_last = gc_row[:, :, C - 1:C]
    decay = jnp.exp(jnp.where(incl, gc_col - gc_row, -jnp.inf))
    kb = k.astype(BF16)
    qkk = jnp.einsum("gik,gjk->gij", jnp.concatenate([q.astype(BF16), kb], axis=1), kb,
                     preferred_element_type=F32)
    p0 = -jnp.where(off_diag, qkk[:, C:] * decay * b_col, 0.0)
    pb = p0.astype(BF16)
    x = eye + p0
    p = bmm(pb, pb)
    for lvl in range(1, n_levels):
        pb = p.astype(BF16)
        if lvl + 1 < n_levels:
            xp = bmm(pb, jnp.concatenate([x, p], axis=2).astype(BF16))
            x = x + xp[:, :, :C]
            p = xp[:, :, C:]
        else:
            x = x + bmm(pb, x.astype(BF16))
    t0 = x.astype(BF16)
    ah, al = split(eye - p0)
    at = bmm(jnp.concatenate([ah, al], axis=1), t0)
    resid = eye - (at[:, :C] + at[:, C:])
    t = t0.astype(F32) + bmm(t0, resid.astype(BF16))
    tb = t * b_row
    tbh, tbl = split(tb)
    vh, vl = split(v)
    uu = bmm(tbh, jnp.concatenate([vh, vl], axis=2))
    u = uu[:, :, :GDN_DV] + uu[:, :, GDN_DV:] + bmm(tbl, vh)
    w = bmm((tb * jnp.exp(gc_row)).astype(BF16), kb).astype(BF16)
    qkd = (qkk[:, :C] * decay).astype(BF16)
    qe = (q * jnp.exp(gc_col)).astype(BF16)
    kd = (k * jnp.exp(g_last - gc_col)).astype(BF16)

    s = s_ref[0]
    r_ = jnp.einsum("hck,hkv->hcv", jnp.concatenate([w, qe], axis=1), s.astype(BF16), preferred_element_type=F32)
    vb = (u - r_[:, :C]).astype(BF16)
    o = r_[:, C:] + jnp.einsum("hij,hjv->hiv", qkd, vb, preferred_element_type=F32)
    s_ref[0] = s * jnp.exp(g_last) + jnp.einsum("hck,hcv->hkv", kd, vb, preferred_element_type=F32)
    o = o * lax.rsqrt(jnp.mean(o * o, axis=-1, keepdims=True) + EPS) * wn_ref[...]
    for h in range(nh):
        z = z_ref[:, h * GDN_DV:(h + 1) * GDN_DV]
        o_ref[:, h * GDN_DV:(h + 1) * GDN_DV] = (o[h] * (z * _sigmoid(z))).astype(BF16)


def gdn_chunk_rule(conv, gates_t, gates_r, qkvz, w_norm, s0, layer, seq_len, chunk):
    m = conv.shape[0]
    b = m // seq_len
    nc = seq_len // chunk
    assert b * seq_len == m and nc * chunk == seq_len
    nqk = GDN_QK_HEADS * GDN_DK
    rows = lambda bi, c: bi * nc + c
    return pl.pallas_call(
        _gdn_rule_kernel,
        grid=(b, nc),
        in_specs=[pl.BlockSpec((chunk, nqk), lambda bi, c: (rows(bi, c), 0)),
                  pl.BlockSpec((chunk, nqk), lambda bi, c: (rows(bi, c), 1)),
                  pl.BlockSpec((chunk, GDN_Z), lambda bi, c: (rows(bi, c), 2 * nqk // GDN_Z)),
                  pl.BlockSpec((chunk, LANES), lambda bi, c: (rows(bi, c), 0)),
                  pl.BlockSpec((1, LANES, chunk), lambda bi, c: (rows(bi, c), 0, 0)),
                  pl.BlockSpec((chunk, GDN_Z), lambda bi, c: (rows(bi, c), GDN_QKV // GDN_Z)),
                  pl.BlockSpec((1, GDN_DV), lambda bi, c: (0, 0)),
                  pl.BlockSpec((1, 1, GDN_V_HEADS, GDN_DK, GDN_DV), lambda bi, c: (layer, bi, 0, 0, 0))],
        out_specs=[pl.BlockSpec((chunk, GDN_Z), lambda bi, c: (rows(bi, c), 0)),
                   pl.BlockSpec((1, GDN_V_HEADS, GDN_DK, GDN_DV), lambda bi, c: (bi, 0, 0, 0))],
        out_shape=[jax.ShapeDtypeStruct((m, GDN_Z), BF16),
                   jax.ShapeDtypeStruct((b, GDN_V_HEADS, GDN_DK, GDN_DV), F32)],
        compiler_params=_cparams("parallel", "arbitrary"),
        name="gdn_chunk_rule",
    )(conv, conv, conv, gates_t, gates_r, qkvz, w_norm.reshape(1, GDN_DV), s0)


def _toeplitz_bias(trow, rows):
    t = jnp.broadcast_to(trow, (rows, trow.shape[1]))
    return pltpu.roll(t, 0, axis=1, stride=1, stride_axis=0)


def _bias_rows(table):
    assert BAND_PAST == 2 * MAX_REL
    far = table[:, 2 * MAX_REL:]
    left = jnp.broadcast_to(far, (table.shape[0], MAX_REL))
    right = jnp.broadcast_to(far, (table.shape[0], 2 * BAND_PAST - 3 * MAX_REL - 1))
    return jnp.concatenate([left, table[:, ::-1], right], axis=1)[:, None, :]


def _att_prompt_kernel(q_ref, kp_ref, kc_ref, vp_ref, vc_ref, trow_ref, o_ref, bias_ref):
    i = pl.program_id(1)
    qb_rows = q_ref.shape[0]
    n_heads, pq, pk = bias_ref.shape
    n_parts = qb_rows // pq

    @pl.when(i == 0)
    def _():
        r = lax.broadcasted_iota(jnp.int32, (pq, pk), 0)
        w = lax.broadcasted_iota(jnp.int32, (pq, pk), 1)
        start = r - r % CHUNK
        in_band = jnp.logical_and(w >= start, w < start + BAND)
        for hd in range(n_heads):
            bias_ref[hd] = jnp.where(in_band, _toeplitz_bias(trow_ref[hd], pq)[:, :pk] * LOG2E, NEG_BIG)

    def body(first_block):
        col = lax.broadcasted_iota(jnp.int32, (pq, pk), 1)
        units = [(hd, p) for hd in range(n_heads) for p in range(n_parts)]
        ks, vs, ss, es, ls = [], [], [], [], []
        for hd in range(n_heads):
            hs = slice(hd * ATT_DH, (hd + 1) * ATT_DH)
            ks.append(jnp.concatenate([kp_ref[:, hs], kc_ref[:, hs]], axis=0).astype(BF16))
            vs.append(jnp.concatenate([vp_ref[:, hs], vc_ref[:, hs]], axis=0).astype(BF16))
        for hd, p in units:
            q = (q_ref[p * pq:(p + 1) * pq, hd * ATT_DH:(hd + 1) * ATT_DH] * (ATT_DH ** -0.5 * LOG2E)).astype(BF16)
            ss.append(_dot_nt(q, ks[hd][p * pq:p * pq + pk]) + bias_ref[hd])
        for (hd, p), s in zip(units, ss):
            if first_block:
                s = jnp.where(col < qb_rows - p * pq, NEG_BIG, s)
            e = jnp.exp2(s - jnp.max(s, axis=-1, keepdims=True))
            es.append(e.astype(BF16))
            ls.append(jnp.sum(e, axis=-1, keepdims=True))
        for (hd, p), e, l in zip(units, es, ls):
            o_ref[p * pq:(p + 1) * pq, hd * ATT_DH:(hd + 1) * ATT_DH] = (
                _dot(e, vs[hd][p * pq:p * pq + pk]) / l).astype(BF16)

    @pl.when(i == 0)
    def _():
        body(True)

    @pl.when(i > 0)
    def _():
        body(False)


def att_prompt(qkv, trows, *, heads_per_step=4):
    l = qkv.shape[0]
    qb = BAND_PAST
    assert l % qb == 0 and ATT_HEADS % heads_per_step == 0
    g = ATT_HEADS // heads_per_step
    wd = heads_per_step * ATT_DH
    prev = lambda i: jnp.maximum(i - 1, 0)
    return pl.pallas_call(
        _att_prompt_kernel,
        grid=(g, l // qb),
        in_specs=[pl.BlockSpec((qb, wd), lambda hh, i: (i, hh)),
                  pl.BlockSpec((qb, wd), lambda hh, i: (prev(i), g + hh)),
                  pl.BlockSpec((qb, wd), lambda hh, i: (i, g + hh)),
                  pl.BlockSpec((qb, wd), lambda hh, i: (prev(i), 2 * g + hh)),
                  pl.BlockSpec((qb, wd), lambda hh, i: (i, 2 * g + hh)),
                  pl.BlockSpec((heads_per_step, 1, 2 * qb), lambda hh, i: (hh, 0, 0))],
        out_specs=pl.BlockSpec((qb, wd), lambda hh, i: (i, hh)),
        out_shape=jax.ShapeDtypeStruct((l, ATT_HEADS * ATT_DH), BF16),
        scratch_shapes=[pltpu.VMEM((heads_per_step, qb // 4, qb // 4 + BAND_PAST), F32)],
        compiler_params=_cparams("parallel", "arbitrary"),
        name="att_prompt",
    )(qkv, qkv, qkv, qkv, qkv, trows)


def _att_sample_kernel(qkv_ref, ck_ref, cv_ref, trow_ref, o_ref):
    d = ATT_HEADS * ATT_DH
    n_new = qkv_ref.shape[0]
    n_old = ck_ref.shape[2]
    heads = range(ATT_HEADS)
    cols = lambda h, section: slice(section * d + h * ATT_DH, section * d + (h + 1) * ATT_DH)
    s1, s2, e1, e2, ls = [], [], [], [], []
    for h in heads:
        q = (qkv_ref[:, cols(h, 0)] * (ATT_DH ** -0.5 * LOG2E)).astype(BF16)
        bias = _toeplitz_bias(trow_ref[h], n_new) * LOG2E
        s1.append(_dot_nt(q, ck_ref[0, 0, :, cols(h, 0)].astype(BF16)) + bias[:, :n_old])
        s2.append(_dot_nt(q, qkv_ref[:, cols(h, 1)].astype(BF16)) + bias[:, n_old:n_old + n_new])
    for h in heads:
        m = jnp.maximum(jnp.max(s1[h], axis=-1, keepdims=True), jnp.max(s2[h], axis=-1, keepdims=True))
        a1 = jnp.exp2(s1[h] - m)
        a2 = jnp.exp2(s2[h] - m)
        ls.append(jnp.sum(a1, axis=-1, keepdims=True) + jnp.sum(a2, axis=-1, keepdims=True))
        e1.append(a1.astype(BF16))
        e2.append(a2.astype(BF16))
    for h in heads:
        o = _dot(e1[h], cv_ref[0, 0, :, cols(h, 0)].astype(BF16)) + _dot(e2[h], qkv_ref[:, cols(h, 2)].astype(BF16))
        o_ref[:, cols(h, 0)] = (o / ls[h]).astype(BF16)


def att_sample(qkv, cache_k, cache_v, layer, trows, seq_len):
    m = qkv.shape[0]
    b = m // seq_len
    r = cache_k.shape[2]
    d = ATT_HEADS * ATT_DH
    assert r == BAND_PAST and r + seq_len <= 2 * BAND_PAST
    return pl.pallas_call(
        _att_sample_kernel,
        grid=(b,),
        in_specs=[pl.BlockSpec((seq_len, 3 * d), lambda i: (i, 0)),
                  pl.BlockSpec((1, 1, r, d), lambda i: (layer, i, 0, 0)),
                  pl.BlockSpec((1, 1, r, d), lambda i: (layer, i, 0, 0)),
                  pl.BlockSpec((ATT_HEADS, 1, 2 * BAND_PAST), lambda i: (0, 0, 0))],
        out_specs=pl.BlockSpec((seq_len, d), lambda i: (i, 0)),
        out_shape=jax.ShapeDtypeStruct((m, d), BF16),
        compiler_params=_cparams("parallel"),
        name="att_sample",
    )(qkv, cache_k, cache_v, trows)


def _gdn_layer(xp, xs, nw, j, w_in_all, w_conv, a_log, dt_bias, w_norm, w_out_all, state_rec_all, state_conv, dec_seq):
    w_in_t = jnp.swapaxes(w_in_all, 1, 2)
    w_main = cast_bf16_transposed(w_in_t, j, 0, GDN_QKV + GDN_Z)
    w_ba = cast_bf16_transposed(w_in_t, j, GDN_QKV + GDN_Z, 2 * GDN_V_HEADS)
    w_out_b = cast_bf16(w_out_all, j)
    gate_params = jnp.zeros((2, LANES), F32)
    gate_params = gate_params.at[0, GDN_V_HEADS:2 * GDN_V_HEADS].set(a_log)
    gate_params = gate_params.at[1, GDN_V_HEADS:2 * GDN_V_HEADS].set(dt_bias)
    outs = []
    for x, seq_len, chunk, conv0, rec0, tr in (
            (xp, xp.shape[0], CHUNK, jnp.zeros((1, CONV_W - 1, GDN_QKV), F32),
             (jnp.zeros((1, 1, GDN_V_HEADS, GDN_DK, GDN_DV), F32), 0), 256),
            (xs, dec_seq, dec_seq, state_conv, (state_rec_all, j), dec_seq)):
        qkvz, ba = norm_proj(x, nw, w_main, w2=w_ba, tm=1024)
        conv8 = jnp.pad(conv0, ((0, 0), (SUBLANES - (CONV_W - 1), 0), (0, 0)))
        conv = gdn_prep(qkvz, conv8, w_conv, seq_len, tr=tr)
        gates_t, gates_r = gdn_gates(ba, gate_params, chunk)
        o, rec = gdn_chunk_rule(conv, gates_t, gates_r, qkvz, w_norm, rec0[0], rec0[1], seq_len, chunk)
        x_new = out_proj(x, o, w_out_b, tm=1024)
        nb = x.shape[0] // seq_len
        new_conv = qkvz.reshape(nb, seq_len, -1)[:, seq_len - (CONV_W - 1):, :GDN_QKV]
        outs.append((x_new, new_conv, rec))
    return outs


def _att_layer(xp, xs, nw, j, w_qkv_all, b_qkv, table, w_o_all, b_o, cache_k, cache_v, dec_seq):
    d = xp.shape[1]
    w_qkv_b = cast_bf16(w_qkv_all, j)
    w_o_b = cast_bf16(w_o_all, j)
    trows = _bias_rows(table)
    qkv_p = norm_proj(xp, nw, w_qkv_b, bias=b_qkv, tm=1024)
    o_p = att_prompt(qkv_p, trows)
    xp_new = out_proj(xp, o_p, w_o_b, bias=b_o, tn=d)
    lp = xp.shape[0]
    keep = min(BAND_PAST, lp)
    kp = qkv_p[lp - keep:, d:2 * d].reshape(1, keep, ATT_HEADS, ATT_DH)
    vp = qkv_p[lp - keep:, 2 * d:].reshape(1, keep, ATT_HEADS, ATT_DH)
    n_att, nb, r = cache_k.shape[:3]
    qkv_s = norm_proj(xs, nw, w_qkv_b, bias=b_qkv)
    o_s = att_sample(qkv_s, cache_k.reshape(n_att, nb, r, d), cache_v.reshape(n_att, nb, r, d), j, trows, dec_seq)
    xs_new = out_proj(xs, o_s, w_o_b, bias=b_o, tn=d)
    ks = qkv_s[:, d:2 * d].reshape(nb, dec_seq, ATT_HEADS, ATT_DH)
    vs = qkv_s[:, 2 * d:].reshape(nb, dec_seq, ATT_HEADS, ATT_DH)
    return (xp_new, kp, vp), (xs_new, ks, vs)


def kernel(x_prompt, x_sample, state_gdn_rec, state_gdn_conv, cache_att_k, cache_att_v, norm_mix, norm_ffn, norm_final, gdn_w_in, gdn_w_conv, gdn_a_log, gdn_dt_bias, gdn_w_norm, gdn_w_out, att_w_qkv, att_b_qkv, att_rel_bias, att_w_o, att_b_o, ffn_w_gate, ffn_w_up, ffn_w_down):
    bp, lp, d = x_prompt.shape
    bs, ls, _ = x_sample.shape
    assert bp == 1
    depth = norm_mix.shape[0]
    xp = x_prompt.reshape(bp * lp, d)
    xs = x_sample.reshape(bs * ls, d)
    p_rec, p_conv, p_k, p_v = [], [], [], []
    s_rec, s_conv, s_k, s_v = [], [], [], []
    for layer in range(depth):
        j = layer // 2
        if layer % 2 == 0:
            (xp, cp, rp), (xs, cs, rs) = _gdn_layer(
                xp, xs, norm_mix[layer], j, gdn_w_in, gdn_w_conv[j], gdn_a_log[j], gdn_dt_bias[j], gdn_w_norm[j],
                gdn_w_out, state_gdn_rec, state_gdn_conv[j], ls)
            p_conv.append(cp)
            p_rec.append(rp)
            s_conv.append(cs)
            s_rec.append(rs)
        else:
            (xp, kp, vp), (xs, kn, vn) = _att_layer(
                xp, xs, norm_mix[layer], j, att_w_qkv, att_b_qkv[j], att_rel_bias[j], att_w_o, att_b_o[j],
                cache_att_k, cache_att_v, ls)
            p_k.append(kp)
            p_v.append(vp)
            s_k.append(kn)
            s_v.append(vn)
        wg = cast_bf16(ffn_w_gate, layer)
        wu = cast_bf16(ffn_w_up, layer)
        wd = cast_bf16(ffn_w_down, layer)
        final_nw = norm_final if layer == depth - 1 else None
        xp, xs = ffn(xp, xs, norm_ffn[layer], wg, wu, wd, final_nw)
    y_prompt = xp.reshape(bp, lp, d)
    y_sample = xs.reshape(bs, ls, d)
    return (y_prompt, y_sample, jnp.stack(p_rec), jnp.stack(p_conv), jnp.stack(p_k), jnp.stack(p_v),
            jnp.stack(s_rec), jnp.stack(s_conv), jnp.stack(s_k), jnp.stack(s_v))
```

```python
import functools
import math

import jax
import jax.numpy as jnp
from jax import lax
from jax.experimental import pallas as pl
from jax.experimental.pallas import tpu as pltpu

EPS = 1e-6
CHUNK = 64
GDN_QK_HEADS = 16
GDN_V_HEADS = 32
GDN_DK = 128
GDN_DV = 128
CONV_W = 4
GDN_QKV = 2 * GDN_QK_HEADS * GDN_DK + GDN_V_HEADS * GDN_DV
GDN_Z = GDN_V_HEADS * GDN_DV
ATT_HEADS = 16
ATT_DH = 128
BAND_PAST = 512
BAND = BAND_PAST + CHUNK
MAX_REL = 256
LANES = 128
SUBLANES = 8
NEG_BIG = -1e30
LOG2E = 1.4426950408889634
VMEM_LIMIT_BYTES = 56 * 1024 * 1024

BF16 = jnp.bfloat16
F32 = jnp.float32
HIGHEST = lax.Precision.HIGHEST


def _cparams(*sem):
    return pltpu.CompilerParams(dimension_semantics=sem, vmem_limit_bytes=VMEM_LIMIT_BYTES)


def _dot(a, b):
    return jnp.dot(a, b, preferred_element_type=F32)


def _dot_nt(a, b):
    return lax.dot_general(a, b, (((1,), (1,)), ((), ())), preferred_element_type=F32)


def _rms(x, w):
    return x * lax.rsqrt(jnp.mean(x * x, axis=-1, keepdims=True) + EPS) * w


def _sigmoid(x):
    return 1.0 / (1.0 + jnp.exp(-x))


def _row_tile(m, pref):
    t = min(m, pref)
    assert m % t == 0, (m, t)
    return t


def _cast_kernel(x_ref, o_ref):
    o_ref[...] = x_ref[0].astype(BF16)


def cast_bf16(w, layer, *, block_bytes=4 * 1024 * 1024):
    _, k, n = w.shape
    tk = max(SUBLANES * 2, min(k, block_bytes // (4 * n) // 16 * 16))
    while k % tk:
        tk -= 16
    return pl.pallas_call(
        _cast_kernel,
        grid=(k // tk,),
        in_specs=[pl.BlockSpec((1, tk, n), lambda i: (layer, i, 0))],
        out_specs=pl.BlockSpec((tk, n), lambda i: (i, 0)),
        out_shape=jax.ShapeDtypeStruct((k, n), BF16),
        compiler_params=_cparams("parallel"),
        name="cast_bf16",
    )(w)


def _cast_transposed_kernel(x_ref, o_ref):
    t = x_ref[0].T.astype(BF16)
    if t.shape[1] < o_ref.shape[1]:
        t = jnp.concatenate([t, jnp.zeros((t.shape[0], o_ref.shape[1] - t.shape[1]), BF16)], axis=1)
    o_ref[...] = t


def cast_bf16_transposed(wt, layer, row0, nrows, *, tr=512):
    _, _, k = wt.shape
    tr = min(tr, nrows)
    assert nrows % tr == 0 and row0 % tr == 0 and tr % SUBLANES == 0
    tc = -(-tr // LANES) * LANES
    return pl.pallas_call(
        _cast_transposed_kernel,
        grid=(nrows // tr,),
        in_specs=[pl.BlockSpec((1, tr, k), lambda i: (layer, row0 // tr + i, 0))],
        out_specs=pl.BlockSpec((k, tc), lambda i: (0, i)),
        out_shape=jax.ShapeDtypeStruct((k, nrows // tr * tc), BF16),
        compiler_params=_cparams("parallel"),
        name="cast_bf16_transposed",
    )(wt)


def _norm_proj_kernel(*refs, has_bias, has_w2):
    x_ref, nw_ref, w_ref = refs[:3]
    pos = 3
    b_ref = w2_ref = None
    if has_bias:
        b_ref = refs[pos]
        pos += 1
    if has_w2:
        w2_ref = refs[pos]
        pos += 1
    o_ref = refs[pos]
    pos += 1
    o2_ref = None
    if has_w2:
        o2_ref = refs[pos]
        pos += 1
    h_ref = refs[pos]

    @pl.when(pl.program_id(1) == 0)
    def _():
        h = _rms(x_ref[...], nw_ref[...]).astype(BF16)
        h_ref[...] = h
        if has_w2:
            o2_ref[...] = _dot(h, w2_ref[...])

    acc = _dot(h_ref[...], w_ref[...])
    if has_bias:
        acc = acc + b_ref[...]
    o_ref[...] = acc


def norm_proj(x, nw, w, bias=None, w2=None, *, tm=512, tn=1024):
    m, d = x.shape
    n = w.shape[1]
    tm = _row_tile(m, tm)
    tn = _row_tile(n, tn)
    in_specs = [pl.BlockSpec((tm, d), lambda i, j: (i, 0)),
                pl.BlockSpec((1, d), lambda i, j: (0, 0)),
                pl.BlockSpec((d, tn), lambda i, j: (0, j))]
    args = [x, nw.reshape(1, d), w]
    if bias is not None:
        in_specs.append(pl.BlockSpec((1, tn), lambda i, j: (0, j)))
        args.append(bias.reshape(1, n))
    out_shape = [jax.ShapeDtypeStruct((m, n), F32)]
    out_specs = [pl.BlockSpec((tm, tn), lambda i, j: (i, j))]
    if w2 is not None:
        n2 = w2.shape[1]
        in_specs.append(pl.BlockSpec((d, n2), lambda i, j: (0, 0)))
        args.append(w2)
        out_shape.append(jax.ShapeDtypeStruct((m, n2), F32))
        out_specs.append(pl.BlockSpec((tm, n2), lambda i, j: (i, 0)))
    res = pl.pallas_call(
        functools.partial(_norm_proj_kernel, has_bias=bias is not None, has_w2=w2 is not None),
        grid=(m // tm, n // tn),
        in_specs=in_specs, out_specs=out_specs, out_shape=out_shape,
        scratch_shapes=[pltpu.VMEM((tm, d), BF16)],
        compiler_params=_cparams("parallel", "arbitrary"),
        name="norm_proj",
    )(*args)
    return res if w2 is not None else res[0]


def _ffn_kernel(x_ref, nw_ref, wg_ref, wu_ref, wd_ref, *rest):
    fw_ref = rest[0] if len(rest) == 3 else None
    o_ref, h_ref = rest[-2:]
    j = pl.program_id(1)

    @pl.when(j == 0)
    def _():
        x = x_ref[...]
        h_ref[...] = _rms(x, nw_ref[...]).astype(BF16)
        o_ref[...] = x

    h = h_ref[...]
    g = _dot(h, wg_ref[...])
    u = _dot(h, wu_ref[...])
    a = (g * _sigmoid(g) * u).astype(BF16)
    o_ref[...] += _dot(a, wd_ref[...])

    if fw_ref is not None:
        @pl.when(j == pl.num_programs(1) - 1)
        def _():
            o_ref[...] = _rms(o_ref[...], fw_ref[...])


def ffn(x, nw, wg, wu, wd, final_nw=None, *, tm=512, tf=512):
    m, d = x.shape
    f = wg.shape[1]
    tm = _row_tile(m, tm)
    tf = _row_tile(f, tf)
    in_specs = [pl.BlockSpec((tm, d), lambda i, j: (i, 0)),
                pl.BlockSpec((1, d), lambda i, j: (0, 0)),
                pl.BlockSpec((d, tf), lambda i, j: (0, j)),
                pl.BlockSpec((d, tf), lambda i, j: (0, j)),
                pl.BlockSpec((tf, d), lambda i, j: (j, 0))]
    args = [x, nw.reshape(1, d), wg, wu, wd]
    if final_nw is not None:
        in_specs.append(pl.BlockSpec((1, d), lambda i, j: (0, 0)))
        args.append(final_nw.reshape(1, d))
    return pl.pallas_call(
        _ffn_kernel,
        grid=(m // tm, f // tf),
        in_specs=in_specs,
        out_specs=pl.BlockSpec((tm, d), lambda i, j: (i, 0)),
        out_shape=jax.ShapeDtypeStruct((m, d), F32),
        scratch_shapes=[pltpu.VMEM((tm, d), BF16)],
        compiler_params=_cparams("parallel", "arbitrary"),
        name="ffn",
    )(*args)


def _out_proj_kernel(*refs, has_bias):
    a_ref, w_ref = refs[:2]
    b_ref = refs[2] if has_bias else None
    res_ref, o_ref = refs[-2:]
    acc = res_ref[...] + _dot(a_ref[...], w_ref[...])
    if has_bias:
        acc = acc + b_ref[...]
    o_ref[...] = acc


def out_proj(res, a, w, bias=None, *, tm=512, tn=1024):
    m, k = a.shape
    n = w.shape[1]
    assert a.dtype == BF16 and w.dtype == BF16
    tm = _row_tile(m, tm)
    tn = _row_tile(n, tn)
    in_specs = [pl.BlockSpec((tm, k), lambda i, j: (i, 0)),
                pl.BlockSpec((k, tn), lambda i, j: (0, j))]
    args = [a, w]
    if bias is not None:
        in_specs.append(pl.BlockSpec((1, tn), lambda i, j: (0, j)))
        args.append(bias.reshape(1, n))
    in_specs.append(pl.BlockSpec((tm, tn), lambda i, j: (i, j)))
    args.append(res)
    return pl.pallas_call(
        functools.partial(_out_proj_kernel, has_bias=bias is not None),
        grid=(m // tm, n // tn),
        in_specs=in_specs,
        out_specs=pl.BlockSpec((tm, tn), lambda i, j: (i, j)),
        out_shape=jax.ShapeDtypeStruct((m, n), F32),
        compiler_params=_cparams("parallel", "parallel"),
        name="out_proj",
    )(*args)


def _gdn_prep_kernel(x_ref, p_ref, st_ref, w_ref, o_ref, *, tiles_per_seq, n_qk_blocks, rb):
    i = pl.program_id(0)
    j = pl.program_id(1)
    tr, tc = x_ref.shape
    w = w_ref[...]
    first = (i % tiles_per_seq) == 0
    prev8 = jnp.where(first, st_ref[0], p_ref[...])

    def strip(halo, body, normalize):
        outs = []
        for c in range(tc // LANES):
            scale = jnp.where(j * (tc // LANES) + c < GDN_QK_HEADS, GDN_DK ** -0.5, 1.0)
            sl = slice(c * LANES, (c + 1) * LANES)
            xx = jnp.concatenate([halo[:, sl], body[:, sl]], axis=0)
            acc = xx * w[CONV_W - 1:CONV_W, sl]
            for s in range(1, CONV_W):
                acc = acc + pltpu.roll(xx, s, axis=0) * w[CONV_W - 1 - s:CONV_W - s, sl]
            a = acc[SUBLANES:]
            a = a * _sigmoid(a)
            if normalize:
                a = a * (lax.rsqrt(jnp.sum(a * a, axis=-1, keepdims=True) + EPS) * scale)
            outs.append(a)
        return jnp.concatenate(outs, axis=1)

    def run(normalize):
        o_ref[:rb, :] = strip(prev8, x_ref[:rb, :], normalize)

        def step(b, carry):
            r0 = pl.multiple_of(b * rb, rb)
            o_ref[pl.ds(r0, rb), :] = strip(x_ref[pl.ds(r0 - SUBLANES, SUBLANES), :], x_ref[pl.ds(r0, rb), :],
                                            normalize)
            return carry

        lax.fori_loop(1, tr // rb, step, 0, unroll=True)

    @pl.when(j < n_qk_blocks)
    def _():
        run(True)

    @pl.when(j >= n_qk_blocks)
    def _():
        run(False)


def gdn_prep(qkvz, conv_state8, w_conv, seq_len, *, tr, tc=4096, rb=32):
    m = qkvz.shape[0]
    assert seq_len % tr == 0 and tr % rb == 0 and rb % SUBLANES == 0
    tiles_per_seq = seq_len // tr
    n_qk_blocks = 2 * GDN_QK_HEADS * GDN_DK // tc
    assert n_qk_blocks * tc == 2 * GDN_QK_HEADS * GDN_DK
    return pl.pallas_call(
        functools.partial(_gdn_prep_kernel, tiles_per_seq=tiles_per_seq, n_qk_blocks=n_qk_blocks, rb=rb),
        grid=(m // tr, GDN_QKV // tc),
        in_specs=[pl.BlockSpec((tr, tc), lambda i, j: (i, j)),
                  pl.BlockSpec((SUBLANES, tc), lambda i, j: (jnp.maximum(i * (tr // SUBLANES) - 1, 0), j)),
                  pl.BlockSpec((1, SUBLANES, tc), lambda i, j: (i // tiles_per_seq, 0, j)),
                  pl.BlockSpec((CONV_W, tc), lambda i, j: (0, j))],
        out_specs=pl.BlockSpec((tr, tc), lambda i, j: (i, j)),
        out_shape=jax.ShapeDtypeStruct((m, GDN_QKV), F32),
        compiler_params=_cparams("parallel", "parallel"),
        name="gdn_prep",
    )(qkvz, qkvz, conv_state8, w_conv)


def _gdn_gates_kernel(ba_ref, gp_ref, t_ref, r_ref, *, chunk):
    n = ba_ref.shape[0] // chunk
    gp = gp_ref[...]
    row = lax.broadcasted_iota(jnp.int32, (chunk, chunk), 0)
    col = lax.broadcasted_iota(jnp.int32, (chunk, chunk), 1)
    tri = jnp.where(row >= col, 1.0, 0.0).astype(F32)
    lane = lax.broadcasted_iota(jnp.int32, (chunk, LANES), 1)
    for ci in range(n):
        rows = slice(ci * chunk, (ci + 1) * chunk)
        ba = ba_ref[rows, :]
        xs = ba + gp[1:2]
        softplus = jnp.maximum(xs, 0.0) + jnp.log1p(jnp.exp(-jnp.abs(xs)))
        g = -jnp.exp(gp[0:1]) * softplus
        gcum = jnp.dot(tri, g, precision=HIGHEST, preferred_element_type=F32)
        t = jnp.where(lane < GDN_V_HEADS, _sigmoid(ba), gcum)
        t_ref[rows, :] = t
        r_ref[ci] = t.T


def gdn_gates(ba, gate_params, chunk, *, chunks_per_step=8):
    m = ba.shape[0]
    nc = m // chunk
    n = min(chunks_per_step, nc)
    assert nc * chunk == m and nc % n == 0
    return pl.pallas_call(
        functools.partial(_gdn_gates_kernel, chunk=chunk),
        grid=(nc // n,),
        in_specs=[pl.BlockSpec((n * chunk, LANES), lambda i: (i, 0)),
                  pl.BlockSpec((2, LANES), lambda i: (0, 0))],
        out_specs=[pl.BlockSpec((n * chunk, LANES), lambda i: (i, 0)),
                   pl.BlockSpec((n, LANES, chunk), lambda i: (i, 0, 0))],
        out_shape=[jax.ShapeDtypeStruct((m, LANES), F32), jax.ShapeDtypeStruct((nc, LANES, chunk), F32)],
        compiler_params=_cparams("parallel"),
        name="gdn_gates",
    )(ba, gate_params)


def _gdn_rule_kernel(q_ref, k_ref, v_ref, t_ref, r_ref, z_ref, wn_ref, s0_ref, o_ref, s_ref):
    C = q_ref.shape[0]
    nh = GDN_V_HEADS
    row = lax.broadcasted_iota(jnp.int32, (C, C), 0)
    col = lax.broadcasted_iota(jnp.int32, (C, C), 1)
    incl = (row >= col)[None]
    off_diag = (row != col)[None]
    eye = jnp.where(row == col, 1.0, 0.0).astype(F32)[None]
    n_levels = int(math.log2(C))
    assert 1 << n_levels == C
    bmm = functools.partial(jnp.einsum, "gij,gjk->gik", preferred_element_type=F32)

    def split(a):
        hi = a.astype(BF16)
        return hi, (a - hi.astype(F32)).astype(BF16)

    @pl.when(pl.program_id(1) == 0)
    def _():
        s_ref[...] = s0_ref[0]

    t_cols = t_ref[...]
    t_rows = r_ref[0]
    b_col = jnp.stack([t_cols[:, r:r + 1] for r in range(nh)])
    gc_col = jnp.stack([t_cols[:, nh + r:nh + r + 1] for r in range(nh)])
    b_row = jnp.stack([t_rows[r:r + 1, :] for r in range(nh)])
    gc_row = jnp.stack([t_rows[nh + r:nh + r + 1, :] for r in range(nh)])
    qk_heads = [(r * GDN_QK_HEADS) // GDN_V_HEADS for r in range(nh)]
    q = jnp.stack([q_ref[:, h * GDN_DK:(h + 1) * GDN_DK] for h in qk_heads])
    k = jnp.stack([k_ref[:, h * GDN_DK:(h + 1) * GDN_DK] for h in qk_heads])
    v = jnp.stack([v_ref[:, r * GDN_DV:(r + 1) * GDN_DV] for r in range(nh)])
    g_last = gc_row[:, :, C - 1:C]
    decay = jnp.exp(jnp.where(incl, gc_col - gc_row, -jnp.inf))
    kb = k.astype(BF16)
    qkk = jnp.einsum("gik,gjk->gij", jnp.concatenate([q.astype(BF16), kb], axis=1), kb,
                     preferred_element_type=F32)
    p0 = -jnp.where(off_diag, qkk[:, C:] * decay * b_col, 0.0)
    pb = p0.astype(BF16)
    x = eye + p0
    p = bmm(pb, pb)
    for lvl in range(1, n_levels):
        pb = p.astype(BF16)
        if lvl + 1 < n_levels:
            xp = bmm(pb, jnp.concatenate([x, p], axis=2).astype(BF16))
            x = x + xp[:, :, :C]
            p = xp[:, :, C:]
        else:
            x = x + bmm(pb, x.astype(BF16))
    t0 = x.astype(BF16)
    ah, al = split(eye - p0)
    at = bmm(jnp.concatenate([ah, al], axis=1), t0)
    resid = eye - (at[:, :C] + at[:, C:])
    t = t0.astype(F32) + bmm(t0, resid.astype(BF16))
    tb = t * b_row
    tbh, tbl = split(tb)
    vh, vl = split(v)
    uu = bmm(tbh, jnp.concatenate([vh, vl], axis=2))
    u = uu[:, :, :GDN_DV] + uu[:, :, GDN_DV:] + bmm(tbl, vh)
    w = bmm((tb * jnp.exp(gc_row)).astype(BF16), kb).astype(BF16)
    qkd = (qkk[:, :C] * decay).astype(BF16)
    qe = (q * jnp.exp(gc_col)).astype(BF16)
    kd = (k * jnp.exp(g_last - gc_col)).astype(BF16)

    s = s_ref[0]
    r_ = jnp.einsum("hck,hkv->hcv", jnp.concatenate([w, qe], axis=1), s.astype(BF16), preferred_element_type=F32)
    vb = (u - r_[:, :C]).astype(BF16)
    o = r_[:, C:] + jnp.einsum("hij,hjv->hiv", qkd, vb, preferred_element_type=F32)
    s_ref[0] = s * jnp.exp(g_last) + jnp.einsum("hck,hcv->hkv", kd, vb, preferred_element_type=F32)
    o = o * lax.rsqrt(jnp.mean(o * o, axis=-1, keepdims=True) + EPS) * wn_ref[...]
    for h in range(nh):
        z = z_ref[:, h * GDN_DV:(h + 1) * GDN_DV]
        o_ref[:, h * GDN_DV:(h + 1) * GDN_DV] = (o[h] * (z * _sigmoid(z))).astype(BF16)


def gdn_chunk_rule(conv, gates_t, gates_r, qkvz, w_norm, s0, layer, seq_len, chunk):
    m = conv.shape[0]
    b = m // seq_len
    nc = seq_len // chunk
    assert b * seq_len == m and nc * chunk == seq_len
    nqk = GDN_QK_HEADS * GDN_DK
    rows = lambda bi, c: bi * nc + c
    return pl.pallas_call(
        _gdn_rule_kernel,
        grid=(b, nc),
        in_specs=[pl.BlockSpec((chunk, nqk), lambda bi, c: (rows(bi, c), 0)),
                  pl.BlockSpec((chunk, nqk), lambda bi, c: (rows(bi, c), 1)),
                  pl.BlockSpec((chunk, GDN_Z), lambda bi, c: (rows(bi, c), 2 * nqk // GDN_Z)),
                  pl.BlockSpec((chunk, LANES), lambda bi, c: (rows(bi, c), 0)),
                  pl.BlockSpec((1, LANES, chunk), lambda bi, c: (rows(bi, c), 0, 0)),
                  pl.BlockSpec((chunk, GDN_Z), lambda bi, c: (rows(bi, c), GDN_QKV // GDN_Z)),
                  pl.BlockSpec((1, GDN_DV), lambda bi, c: (0, 0)),
                  pl.BlockSpec((1, 1, GDN_V_HEADS, GDN_DK, GDN_DV), lambda bi, c: (layer, bi, 0, 0, 0))],
        out_specs=[pl.BlockSpec((chunk, GDN_Z), lambda bi, c: (rows(bi, c), 0)),
                   pl.BlockSpec((1, GDN_V_HEADS, GDN_DK, GDN_DV), lambda bi, c: (bi, 0, 0, 0))],
        out_shape=[jax.ShapeDtypeStruct((m, GDN_Z), BF16),
                   jax.ShapeDtypeStruct((b, GDN_V_HEADS, GDN_DK, GDN_DV), F32)],
        compiler_params=_cparams("parallel", "arbitrary"),
        name="gdn_chunk_rule",
    )(conv, conv, conv, gates_t, gates_r, qkvz, w_norm.reshape(1, GDN_DV), s0)


def _toeplitz_bias(trow, rows):
    t = jnp.broadcast_to(trow, (rows, trow.shape[1]))
    return pltpu.roll(t, 0, axis=1, stride=1, stride_axis=0)


def _bias_rows(table):
    assert BAND_PAST == 2 * MAX_REL
    far = table[:, 2 * MAX_REL:]
    left = jnp.broadcast_to(far, (table.shape[0], MAX_REL))
    right = jnp.broadcast_to(far, (table.shape[0], 2 * BAND_PAST - 3 * MAX_REL - 1))
    return jnp.concatenate([left, table[:, ::-1], right], axis=1)[:, None, :]


def _att_prompt_kernel(q_ref, kp_ref, kc_ref, vp_ref, vc_ref, trow_ref, o_ref, bias_ref):
    i = pl.program_id(1)
    qb_rows = q_ref.shape[0]
    n_heads, pq, pk = bias_ref.shape
    n_parts = qb_rows // pq

    @pl.when(i == 0)
    def _():
        r = lax.broadcasted_iota(jnp.int32, (pq, pk), 0)
        w = lax.broadcasted_iota(jnp.int32, (pq, pk), 1)
        start = r - r % CHUNK
        in_band = jnp.logical_and(w >= start, w < start + BAND)
        for hd in range(n_heads):
            bias_ref[hd] = jnp.where(in_band, _toeplitz_bias(trow_ref[hd], pq)[:, :pk] * LOG2E, NEG_BIG)

    def body(first_block):
        col = lax.broadcasted_iota(jnp.int32, (pq, pk), 1)
        units = [(hd, p) for hd in range(n_heads) for p in range(n_parts)]
        ks, vs, ss, es, ls = [], [], [], [], []
        for hd in range(n_heads):
            hs = slice(hd * ATT_DH, (hd + 1) * ATT_DH)
            ks.append(jnp.concatenate([kp_ref[:, hs], kc_ref[:, hs]], axis=0).astype(BF16))
            vs.append(jnp.concatenate([vp_ref[:, hs], vc_ref[:, hs]], axis=0).astype(BF16))
        for hd, p in units:
            q = (q_ref[p * pq:(p + 1) * pq, hd * ATT_DH:(hd + 1) * ATT_DH] * (ATT_DH ** -0.5 * LOG2E)).astype(BF16)
            ss.append(_dot_nt(q, ks[hd][p * pq:p * pq + pk]) + bias_ref[hd])
        for (hd, p), s in zip(units, ss):
            if first_block:
                s = jnp.where(col < qb_rows - p * pq, NEG_BIG, s)
            e = jnp.exp2(s - jnp.max(s, axis=-1, keepdims=True))
            es.append(e.astype(BF16))
            ls.append(jnp.sum(e, axis=-1, keepdims=True))
        for (hd, p), e, l in zip(units, es, ls):
            o_ref[p * pq:(p + 1) * pq, hd * ATT_DH:(hd + 1) * ATT_DH] = (
                _dot(e, vs[hd][p * pq:p * pq + pk]) / l).astype(BF16)

    @pl.when(i == 0)
    def _():
        body(True)

    @pl.when(i > 0)
    def _():
        body(False)


def att_prompt(qkv, trows, *, heads_per_step=4):
    l = qkv.shape[0]
    qb = BAND_PAST
    assert l % qb == 0 and ATT_HEADS % heads_per_step == 0
    g = ATT_HEADS // heads_per_step
    wd = heads_per_step * ATT_DH
    prev = lambda i: jnp.maximum(i - 1, 0)
    return pl.pallas_call(
        _att_prompt_kernel,
        grid=(g, l // qb),
        in_specs=[pl.BlockSpec((qb, wd), lambda hh, i: (i, hh)),
                  pl.BlockSpec((qb, wd), lambda hh, i: (prev(i), g + hh)),
                  pl.BlockSpec((qb, wd), lambda hh, i: (i, g + hh)),
                  pl.BlockSpec((qb, wd), lambda hh, i: (prev(i), 2 * g + hh)),
                  pl.BlockSpec((qb, wd), lambda hh, i: (i, 2 * g + hh)),
                  pl.BlockSpec((heads_per_step, 1, 2 * qb), lambda hh, i: (hh, 0, 0))],
        out_specs=pl.BlockSpec((qb, wd), lambda hh, i: (i, hh)),
        out_shape=jax.ShapeDtypeStruct((l, ATT_HEADS * ATT_DH), BF16),
        scratch_shapes=[pltpu.VMEM((heads_per_step, qb // 4, qb // 4 + BAND_PAST), F32)],
        compiler_params=_cparams("parallel", "arbitrary"),
        name="att_prompt",
    )(qkv, qkv, qkv, qkv, qkv, trows)


def _att_sample_kernel(qkv_ref, ck_ref, cv_ref, trow_ref, o_ref):
    d = ATT_HEADS * ATT_DH
    n_new = qkv_ref.shape[0]
    n_old = ck_ref.shape[2]
    ck_t = jnp.swapaxes(ck_ref[0, 0], 0, 1)
    cv_t = jnp.swapaxes(cv_ref[0, 0], 0, 1)
    heads = range(ATT_HEADS)
    cols = lambda h, section: slice(section * d + h * ATT_DH, section * d + (h + 1) * ATT_DH)
    s1, s2, e1, e2, ls = [], [], [], [], []
    for h in heads:
        q = (qkv_ref[:, cols(h, 0)] * (ATT_DH ** -0.5 * LOG2E)).astype(BF16)
        bias = _toeplitz_bias(trow_ref[h], n_new) * LOG2E
        s1.append(_dot_nt(q, ck_t[h].astype(BF16)) + bias[:, :n_old])
        s2.append(_dot_nt(q, qkv_ref[:, cols(h, 1)].astype(BF16)) + bias[:, n_old:n_old + n_new])
    for h in heads:
        m = jnp.maximum(jnp.max(s1[h], axis=-1, keepdims=True), jnp.max(s2[h], axis=-1, keepdims=True))
        a1 = jnp.exp2(s1[h] - m)
        a2 = jnp.exp2(s2[h] - m)
        ls.append(jnp.sum(a1, axis=-1, keepdims=True) + jnp.sum(a2, axis=-1, keepdims=True))
        e1.append(a1.astype(BF16))
        e2.append(a2.astype(BF16))
    for h in heads:
        o = _dot(e1[h], cv_t[h].astype(BF16)) + _dot(e2[h], qkv_ref[:, cols(h, 2)].astype(BF16))
        o_ref[:, cols(h, 0)] = (o / ls[h]).astype(BF16)


def att_sample(qkv, cache_k, cache_v, layer, trows, seq_len):
    m = qkv.shape[0]
    b = m // seq_len
    r = cache_k.shape[2]
    d = ATT_HEADS * ATT_DH
    assert r == BAND_PAST and r + seq_len <= 2 * BAND_PAST
    return pl.pallas_call(
        _att_sample_kernel,
        grid=(b,),
        in_specs=[pl.BlockSpec((seq_len, 3 * d), lambda i: (i, 0)),
                  pl.BlockSpec((1, 1, r, ATT_HEADS, ATT_DH), lambda i: (layer, i, 0, 0, 0)),
                  pl.BlockSpec((1, 1, r, ATT_HEADS, ATT_DH), lambda i: (layer, i, 0, 0, 0)),
                  pl.BlockSpec((ATT_HEADS, 1, 2 * BAND_PAST), lambda i: (0, 0, 0))],
        out_specs=pl.BlockSpec((seq_len, d), lambda i: (i, 0)),
        out_shape=jax.ShapeDtypeStruct((m, d), BF16),
        compiler_params=_cparams("parallel"),
        name="att_sample",
    )(qkv, cache_k, cache_v, trows)


def _gdn_layer(xp, xs, nw, j, w_in_all, w_conv, a_log, dt_bias, w_norm, w_out_all, state_rec_all, state_conv, dec_seq):
    w_in_t = jnp.swapaxes(w_in_all, 1, 2)
    w_main = cast_bf16_transposed(w_in_t, j, 0, GDN_QKV + GDN_Z)
    w_ba = cast_bf16_transposed(w_in_t, j, GDN_QKV + GDN_Z, 2 * GDN_V_HEADS)
    w_out_b = cast_bf16(w_out_all, j)
    gate_params = jnp.zeros((2, LANES), F32)
    gate_params = gate_params.at[0, GDN_V_HEADS:2 * GDN_V_HEADS].set(a_log)
    gate_params = gate_params.at[1, GDN_V_HEADS:2 * GDN_V_HEADS].set(dt_bias)
    outs = []
    for x, seq_len, chunk, conv0, rec0, tr in (
            (xp, xp.shape[0], CHUNK, jnp.zeros((1, CONV_W - 1, GDN_QKV), F32),
             (jnp.zeros((1, 1, GDN_V_HEADS, GDN_DK, GDN_DV), F32), 0), 256),
            (xs, dec_seq, dec_seq, state_conv, (state_rec_all, j), dec_seq)):
        qkvz, ba = norm_proj(x, nw, w_main, w2=w_ba, tm=1024)
        conv8 = jnp.pad(conv0, ((0, 0), (SUBLANES - (CONV_W - 1), 0), (0, 0)))
        conv = gdn_prep(qkvz, conv8, w_conv, seq_len, tr=tr)
        gates_t, gates_r = gdn_gates(ba, gate_params, chunk)
        o, rec = gdn_chunk_rule(conv, gates_t, gates_r, qkvz, w_norm, rec0[0], rec0[1], seq_len, chunk)
        x_new = out_proj(x, o, w_out_b, tm=1024)
        nb = x.shape[0] // seq_len
        new_conv = qkvz.reshape(nb, seq_len, -1)[:, seq_len - (CONV_W - 1):, :GDN_QKV]
        outs.append((x_new, new_conv, rec))
    return outs


def _att_layer(xp, xs, nw, j, w_qkv_all, b_qkv, table, w_o_all, b_o, cache_k, cache_v, dec_seq):
    d = xp.shape[1]
    w_qkv_b = cast_bf16(w_qkv_all, j)
    w_o_b = cast_bf16(w_o_all, j)
    trows = _bias_rows(table)
    qkv_p = norm_proj(xp, nw, w_qkv_b, bias=b_qkv, tm=1024)
    o_p = att_prompt(qkv_p, trows)
    xp_new = out_proj(xp, o_p, w_o_b, bias=b_o, tn=d)
    lp = xp.shape[0]
    keep = min(BAND_PAST, lp)
    kp = qkv_p[lp - keep:, d:2 * d].reshape(1, keep, ATT_HEADS, ATT_DH)
    vp = qkv_p[lp - keep:, 2 * d:].reshape(1, keep, ATT_HEADS, ATT_DH)
    nb = cache_k.shape[1]
    qkv_s = norm_proj(xs, nw, w_qkv_b, bias=b_qkv)
    o_s = att_sample(qkv_s, cache_k, cache_v, j, trows, dec_seq)
    xs_new = out_proj(xs, o_s, w_o_b, bias=b_o, tn=d)
    ks = qkv_s[:, d:2 * d].reshape(nb, dec_seq, ATT_HEADS, ATT_DH)
    vs = qkv_s[:, 2 * d:].reshape(nb, dec_seq, ATT_HEADS, ATT_DH)
    return (xp_new, kp, vp), (xs_new, ks, vs)


def kernel(x_prompt, x_sample, state_gdn_rec, state_gdn_conv, cache_att_k, cache_att_v, norm_mix, norm_ffn, norm_final, gdn_w_in, gdn_w_conv, gdn_a_log, gdn_dt_bias, gdn_w_norm, gdn_w_out, att_w_qkv, att_b_qkv, att_rel_bias, att_w_o, att_b_o, ffn_w_gate, ffn_w_up, ffn_w_down):
    bp, lp, d = x_prompt.shape
    bs, ls, _ = x_sample.shape
    assert bp == 1
    depth = norm_mix.shape[0]
    xp = x_prompt.reshape(bp * lp, d)
    xs = x_sample.reshape(bs * ls, d)
    p_rec, p_conv, p_k, p_v = [], [], [], []
    s_rec, s_conv, s_k, s_v = [], [], [], []
    for layer in range(depth):
        j = layer // 2
        if layer % 2 == 0:
            (xp, cp, rp), (xs, cs, rs) = _gdn_layer(
                xp, xs, norm_mix[layer], j, gdn_w_in, gdn_w_conv[j], gdn_a_log[j], gdn_dt_bias[j], gdn_w_norm[j],
                gdn_w_out, state_gdn_rec, state_gdn_conv[j], ls)
            p_conv.append(cp)
            p_rec.append(rp)
            s_conv.append(cs)
            s_rec.append(rs)
        else:
            (xp, kp, vp), (xs, kn, vn) = _att_layer(
                xp, xs, norm_mix[layer], j, att_w_qkv, att_b_qkv[j], att_rel_bias[j], att_w_o, att_b_o[j],
                cache_att_k, cache_att_v, ls)
            p_k.append(kp)
            p_v.append(vp)
            s_k.append(kn)
            s_v.append(vn)
        wg = cast_bf16(ffn_w_gate, layer)
        wu = cast_bf16(ffn_w_up, layer)
        wd = cast_bf16(ffn_w_down, layer)
        final_nw = norm_final if layer == depth - 1 else None
        xp = ffn(xp, norm_ffn[layer], wg, wu, wd, final_nw)
        xs = ffn(xs, norm_ffn[layer], wg, wu, wd, final_nw)
    y_prompt = xp.reshape(bp, lp, d)
    y_sample = xs.reshape(bs, ls, d)
    return (y_prompt, y_sample, jnp.stack(p_rec), jnp.stack(p_conv), jnp.stack(p_k), jnp.stack(p_v),
            jnp.stack(s_rec), jnp.stack(s_conv), jnp.stack(s_k), jnp.stack(s_v))
```

```python
import functools
import math

import jax
import jax.numpy as jnp
from jax import lax
from jax.experimental import pallas as pl
from jax.experimental.pallas import tpu as pltpu

EPS = 1e-6
CHUNK = 64
GDN_QK_HEADS = 16
GDN_V_HEADS = 32
GDN_DK = 128
GDN_DV = 128
CONV_W = 4
GDN_QKV = 2 * GDN_QK_HEADS * GDN_DK + GDN_V_HEADS * GDN_DV
GDN_Z = GDN_V_HEADS * GDN_DV
ATT_HEADS = 16
ATT_DH = 128
BAND_PAST = 512
BAND = BAND_PAST + CHUNK
MAX_REL = 256
LANES = 128
SUBLANES = 8
NEG_BIG = -1e30
LOG2E = 1.4426950408889634
VMEM_LIMIT_BYTES = 56 * 1024 * 1024

BF16 = jnp.bfloat16
F32 = jnp.float32
HIGHEST = lax.Precision.HIGHEST


def _cparams(*sem):
    return pltpu.CompilerParams(dimension_semantics=sem, vmem_limit_bytes=VMEM_LIMIT_BYTES)


def _dot(a, b):
    return jnp.dot(a, b, preferred_element_type=F32)


def _dot_nt(a, b):
    return lax.dot_general(a, b, (((1,), (1,)), ((), ())), preferred_element_type=F32)


def _rms(x, w):
    return x * lax.rsqrt(jnp.mean(x * x, axis=-1, keepdims=True) + EPS) * w


def _sigmoid(x):
    return 1.0 / (1.0 + jnp.exp(-x))


def _row_tile(m, pref):
    t = min(m, pref)
    assert m % t == 0, (m, t)
    return t


def _as_bf16_weight(w_ref, transposed):
    if w_ref.dtype == BF16:
        return w_ref[...]
    w = w_ref[0].T if transposed else w_ref[0]
    return w.astype(BF16)


def _norm_proj_kernel(*refs, has_bias, has_w2, emit, transposed):
    x_ref, nw_ref, w_ref = refs[:3]
    refs = list(refs[3:])
    b_ref = refs.pop(0) if has_bias else None
    w2_ref = refs.pop(0) if has_w2 else None
    o_ref = refs.pop(0)
    o2_ref = refs.pop(0) if has_w2 else None
    wb_ref = refs.pop(0) if emit else None
    w2b_ref = refs.pop(0) if emit and has_w2 else None
    h_ref = refs.pop(0)

    @pl.when(pl.program_id(1) == 0)
    def _():
        h = _rms(x_ref[...], nw_ref[...]).astype(BF16)
        h_ref[...] = h
        if has_w2:
            w2 = _as_bf16_weight(w2_ref, transposed)
            if w2.shape[1] < o2_ref.shape[1]:
                w2 = jnp.concatenate([w2, jnp.zeros((w2.shape[0], o2_ref.shape[1] - w2.shape[1]), BF16)], axis=1)
            o2_ref[...] = _dot(h, w2)
            if emit:
                w2b_ref[...] = w2

    w = _as_bf16_weight(w_ref, transposed)
    if emit:
        wb_ref[...] = w
    acc = _dot(h_ref[...], w)
    if has_bias:
        acc = acc + b_ref[...]
    o_ref[...] = acc


def norm_proj(x, nw, w, bias=None, w2=None, *, tm=512, tn=1024):
    m, d = x.shape
    n = w.shape[1]
    tm = _row_tile(m, tm)
    tn = _row_tile(n, tn)
    in_specs = [pl.BlockSpec((tm, d), lambda i, j: (i, 0)),
                pl.BlockSpec((1, d), lambda i, j: (0, 0)),
                pl.BlockSpec((d, tn), lambda i, j: (0, j))]
    args = [x, nw.reshape(1, d), w]
    if bias is not None:
        in_specs.append(pl.BlockSpec((1, tn), lambda i, j: (0, j)))
        args.append(bias.reshape(1, n))
    out_shape = [jax.ShapeDtypeStruct((m, n), F32)]
    out_specs = [pl.BlockSpec((tm, tn), lambda i, j: (i, j))]
    if w2 is not None:
        n2 = w2.shape[1]
        in_specs.append(pl.BlockSpec((d, n2), lambda i, j: (0, 0)))
        args.append(w2)
        out_shape.append(jax.ShapeDtypeStruct((m, n2), F32))
        out_specs.append(pl.BlockSpec((tm, n2), lambda i, j: (i, 0)))
    res = pl.pallas_call(
        functools.partial(_norm_proj_kernel, has_bias=bias is not None, has_w2=w2 is not None, emit=False,
                          transposed=False),
        grid=(m // tm, n // tn),
        in_specs=in_specs, out_specs=out_specs, out_shape=out_shape,
        scratch_shapes=[pltpu.VMEM((tm, d), BF16)],
        compiler_params=_cparams("parallel", "arbitrary"),
        name="norm_proj",
    )(*args)
    return res if w2 is not None else res[0]


def norm_proj_cast(x, nw, w_all, layer, n, bias=None, n2=0, *, transposed=False, tn=1024):
    m, d = x.shape
    tn = _row_tile(n, tn)
    if transposed:
        w_spec = pl.BlockSpec((1, tn, d), lambda i, j: (layer, j, 0))
        w2_spec = pl.BlockSpec((1, n2, d), lambda i, j: (layer, n // max(n2, 1), 0))
    else:
        w_spec = pl.BlockSpec((1, d, tn), lambda i, j: (layer, 0, j))
        w2_spec = None
    assert n2 == 0 or (transposed and n % n2 == 0 and n2 % SUBLANES == 0 and n2 <= LANES)
    in_specs = [pl.BlockSpec((m, d), lambda i, j: (0, 0)), pl.BlockSpec((1, d), lambda i, j: (0, 0)), w_spec]
    args = [x, nw.reshape(1, d), w_all]
    if bias is not None:
        in_specs.append(pl.BlockSpec((1, tn), lambda i, j: (0, j)))
        args.append(bias.reshape(1, n))
    out_shape = [jax.ShapeDtypeStruct((m, n), F32)]
    out_specs = [pl.BlockSpec((m, tn), lambda i, j: (0, j))]
    if n2:
        in_specs.append(w2_spec)
        args.append(w_all)
        out_shape.append(jax.ShapeDtypeStruct((m, LANES), F32))
        out_specs.append(pl.BlockSpec((m, LANES), lambda i, j: (0, 0)))
    out_shape.append(jax.ShapeDtypeStruct((d, n), BF16))
    out_specs.append(pl.BlockSpec((d, tn), lambda i, j: (0, j)))
    if n2:
        out_shape.append(jax.ShapeDtypeStruct((d, LANES), BF16))
        out_specs.append(pl.BlockSpec((d, LANES), lambda i, j: (0, 0)))
    return pl.pallas_call(
        functools.partial(_norm_proj_kernel, has_bias=bias is not None, has_w2=bool(n2), emit=True,
                          transposed=transposed),
        grid=(1, n // tn),
        in_specs=in_specs, out_specs=out_specs, out_shape=out_shape,
        scratch_shapes=[pltpu.VMEM((m, d), BF16)],
        compiler_params=_cparams("arbitrary", "arbitrary"),
        name="norm_proj_cast",
    )(*args)


def _ffn_kernel(x_ref, nw_ref, wg_ref, wu_ref, wd_ref, *rest, emit):
    rest = list(rest)
    h_ref = rest.pop()
    wb_refs = [rest.pop(), rest.pop(), rest.pop()][::-1] if emit else None
    o_ref = rest.pop()
    fw_ref = rest.pop() if rest else None
    j = pl.program_id(1)

    @pl.when(j == 0)
    def _():
        x = x_ref[...]
        h_ref[...] = _rms(x, nw_ref[...]).astype(BF16)
        o_ref[...] = x

    wg, wu, wd = (_as_bf16_weight(r, False) for r in (wg_ref, wu_ref, wd_ref))
    if emit:
        for wb_ref, w in zip(wb_refs, (wg, wu, wd)):
            wb_ref[...] = w
    h = h_ref[...]
    g = _dot(h, wg)
    u = _dot(h, wu)
    a = (g * _sigmoid(g) * u).astype(BF16)
    o_ref[...] += _dot(a, wd)

    if fw_ref is not None:
        @pl.when(j == pl.num_programs(1) - 1)
        def _():
            o_ref[...] = _rms(o_ref[...], fw_ref[...])


def ffn(x, nw, wg, wu, wd, final_nw=None, *, tm=512, tf=512):
    m, d = x.shape
    f = wg.shape[1]
    tm = _row_tile(m, tm)
    tf = _row_tile(f, tf)
    in_specs = [pl.BlockSpec((tm, d), lambda i, j: (i, 0)),
                pl.BlockSpec((1, d), lambda i, j: (0, 0)),
                pl.BlockSpec((d, tf), lambda i, j: (0, j)),
                pl.BlockSpec((d, tf), lambda i, j: (0, j)),
                pl.BlockSpec((tf, d), lambda i, j: (j, 0))]
    args = [x, nw.reshape(1, d), wg, wu, wd]
    if final_nw is not None:
        in_specs.append(pl.BlockSpec((1, d), lambda i, j: (0, 0)))
        args.append(final_nw.reshape(1, d))
    return pl.pallas_call(
        functools.partial(_ffn_kernel, emit=False),
        grid=(m // tm, f // tf),
        in_specs=in_specs,
        out_specs=pl.BlockSpec((tm, d), lambda i, j: (i, 0)),
        out_shape=jax.ShapeDtypeStruct((m, d), F32),
        scratch_shapes=[pltpu.VMEM((tm, d), BF16)],
        compiler_params=_cparams("parallel", "arbitrary"),
        name="ffn",
    )(*args)


def ffn_cast(x, nw, wg_all, wu_all, wd_all, layer, final_nw=None, *, tf=256):
    m, d = x.shape
    f = wg_all.shape[2]
    tf = _row_tile(f, tf)
    in_specs = [pl.BlockSpec((m, d), lambda i, j: (0, 0)),
                pl.BlockSpec((1, d), lambda i, j: (0, 0)),
                pl.BlockSpec((1, d, tf), lambda i, j: (layer, 0, j)),
                pl.BlockSpec((1, d, tf), lambda i, j: (layer, 0, j)),
                pl.BlockSpec((1, tf, d), lambda i, j: (layer, j, 0))]
    args = [x, nw.reshape(1, d), wg_all, wu_all, wd_all]
    if final_nw is not None:
        in_specs.append(pl.BlockSpec((1, d), lambda i, j: (0, 0)))
        args.append(final_nw.reshape(1, d))
    return pl.pallas_call(
        functools.partial(_ffn_kernel, emit=True),
        grid=(1, f // tf),
        in_specs=in_specs,
        out_specs=[pl.BlockSpec((m, d), lambda i, j: (0, 0)),
                   pl.BlockSpec((d, tf), lambda i, j: (0, j)),
                   pl.BlockSpec((d, tf), lambda i, j: (0, j)),
                   pl.BlockSpec((tf, d), lambda i, j: (j, 0))],
        out_shape=[jax.ShapeDtypeStruct((m, d), F32), jax.ShapeDtypeStruct((d, f), BF16),
                   jax.ShapeDtypeStruct((d, f), BF16), jax.ShapeDtypeStruct((f, d), BF16)],
        scratch_shapes=[pltpu.VMEM((m, d), BF16)],
        compiler_params=_cparams("arbitrary", "arbitrary"),
        name="ffn_cast",
    )(*args)


def _out_proj_kernel(*refs, has_bias, emit):
    a_ref, w_ref = refs[:2]
    b_ref = refs[2] if has_bias else None
    res_ref, o_ref = refs[3:5] if has_bias else refs[2:4]
    w = _as_bf16_weight(w_ref, False)
    if emit:
        refs[-1][...] = w
    acc = res_ref[...] + _dot(a_ref[...], w)
    if has_bias:
        acc = acc + b_ref[...]
    o_ref[...] = acc


def out_proj(res, a, w, bias=None, *, tm=512, tn=1024):
    m, k = a.shape
    n = w.shape[1]
    assert a.dtype == BF16 and w.dtype == BF16
    tm = _row_tile(m, tm)
    tn = _row_tile(n, tn)
    in_specs = [pl.BlockSpec((tm, k), lambda i, j: (i, 0)),
                pl.BlockSpec((k, tn), lambda i, j: (0, j))]
    args = [a, w]
    if bias is not None:
        in_specs.append(pl.BlockSpec((1, tn), lambda i, j: (0, j)))
        args.append(bias.reshape(1, n))
    in_specs.append(pl.BlockSpec((tm, tn), lambda i, j: (i, j)))
    args.append(res)
    return pl.pallas_call(
        functools.partial(_out_proj_kernel, has_bias=bias is not None, emit=False),
        grid=(m // tm, n // tn),
        in_specs=in_specs,
        out_specs=pl.BlockSpec((tm, tn), lambda i, j: (i, j)),
        out_shape=jax.ShapeDtypeStruct((m, n), F32),
        compiler_params=_cparams("parallel", "parallel"),
        name="out_proj",
    )(*args)


def out_proj_cast(res, a, w_all, layer, bias=None, *, tn=512):
    m, k = a.shape
    n = w_all.shape[2]
    assert a.dtype == BF16
    tn = _row_tile(n, tn)
    in_specs = [pl.BlockSpec((m, k), lambda i, j: (0, 0)),
                pl.BlockSpec((1, k, tn), lambda i, j: (layer, 0, j))]
    args = [a, w_all]
    if bias is not None:
        in_specs.append(pl.BlockSpec((1, tn), lambda i, j: (0, j)))
        args.append(bias.reshape(1, n))
    in_specs.append(pl.BlockSpec((m, tn), lambda i, j: (0, j)))
    args.append(res)
    return pl.pallas_call(
        functools.partial(_out_proj_kernel, has_bias=bias is not None, emit=True),
        grid=(1, n // tn),
        in_specs=in_specs,
        out_specs=[pl.BlockSpec((m, tn), lambda i, j: (0, j)), pl.BlockSpec((k, tn), lambda i, j: (0, j))],
        out_shape=[jax.ShapeDtypeStruct((m, n), F32), jax.ShapeDtypeStruct((k, n), BF16)],
        compiler_params=_cparams("arbitrary", "arbitrary"),
        name="out_proj_cast",
    )(*args)


def _gdn_prep_kernel(x_ref, p_ref, st_ref, w_ref, o_ref, *, tiles_per_seq, n_qk_blocks, rb):
    i = pl.program_id(0)
    j = pl.program_id(1)
    tr, tc = x_ref.shape
    w = w_ref[...]
    first = (i % tiles_per_seq) == 0
    prev8 = jnp.where(first, st_ref[0], p_ref[...])

    def strip(halo, body, normalize):
        outs = []
        for c in range(tc // LANES):
            scale = jnp.where(j * (tc // LANES) + c < GDN_QK_HEADS, GDN_DK ** -0.5, 1.0)
            sl = slice(c * LANES, (c + 1) * LANES)
            xx = jnp.concatenate([halo[:, sl], body[:, sl]], axis=0)
            acc = xx * w[CONV_W - 1:CONV_W, sl]
            for s in range(1, CONV_W):
                acc = acc + pltpu.roll(xx, s, axis=0) * w[CONV_W - 1 - s:CONV_W - s, sl]
            a = acc[SUBLANES:]
            a = a * _sigmoid(a)
            if normalize:
                a = a * (lax.rsqrt(jnp.sum(a * a, axis=-1, keepdims=True) + EPS) * scale)
            outs.append(a)
        return jnp.concatenate(outs, axis=1)

    def run(normalize):
        o_ref[:rb, :] = strip(prev8, x_ref[:rb, :], normalize)

        def step(b, carry):
            r0 = pl.multiple_of(b * rb, rb)
            o_ref[pl.ds(r0, rb), :] = strip(x_ref[pl.ds(r0 - SUBLANES, SUBLANES), :], x_ref[pl.ds(r0, rb), :],
                                            normalize)
            return carry

        lax.fori_loop(1, tr // rb, step, 0, unroll=True)

    @pl.when(j < n_qk_blocks)
    def _():
        run(True)

    @pl.when(j >= n_qk_blocks)
    def _():
        run(False)


def gdn_prep(qkvz, conv_state8, w_conv, seq_len, *, tr, tc=4096, rb=32):
    m = qkvz.shape[0]
    assert seq_len % tr == 0 and tr % rb == 0 and rb % SUBLANES == 0
    tiles_per_seq = seq_len // tr
    n_qk_blocks = 2 * GDN_QK_HEADS * GDN_DK // tc
    assert n_qk_blocks * tc == 2 * GDN_QK_HEADS * GDN_DK
    return pl.pallas_call(
        functools.partial(_gdn_prep_kernel, tiles_per_seq=tiles_per_seq, n_qk_blocks=n_qk_blocks, rb=rb),
        grid=(m // tr, GDN_QKV // tc),
        in_specs=[pl.BlockSpec((tr, tc), lambda i, j: (i, j)),
                  pl.BlockSpec((SUBLANES, tc), lambda i, j: (jnp.maximum(i * (tr // SUBLANES) - 1, 0), j)),
                  pl.BlockSpec((1, SUBLANES, tc), lambda i, j: (i // tiles_per_seq, 0, j)),
                  pl.BlockSpec((CONV_W, tc), lambda i, j: (0, j))],
        out_specs=pl.BlockSpec((tr, tc), lambda i, j: (i, j)),
        out_shape=jax.ShapeDtypeStruct((m, GDN_QKV), F32),
        compiler_params=_cparams("parallel", "parallel"),
        name="gdn_prep",
    )(qkvz, qkvz, conv_state8, w_conv)


def _gdn_gates_kernel(ba_ref, gp_ref, t_ref, r_ref, *, chunk):
    n = ba_ref.shape[0] // chunk
    gp = gp_ref[...]
    row = lax.broadcasted_iota(jnp.int32, (chunk, chunk), 0)
    col = lax.broadcasted_iota(jnp.int32, (chunk, chunk), 1)
    tri = jnp.where(row >= col, 1.0, 0.0).astype(F32)
    lane = lax.broadcasted_iota(jnp.int32, (chunk, LANES), 1)
    for ci in range(n):
        rows = slice(ci * chunk, (ci + 1) * chunk)
        ba = ba_ref[rows, :]
        xs = ba + gp[1:2]
        softplus = jnp.maximum(xs, 0.0) + jnp.log1p(jnp.exp(-jnp.abs(xs)))
        g = -jnp.exp(gp[0:1]) * softplus
        gcum = jnp.dot(tri, g, precision=HIGHEST, preferred_element_type=F32)
        t = jnp.where(lane < GDN_V_HEADS, _sigmoid(ba), gcum)
        t_ref[rows, :] = t
        r_ref[ci] = t.T


def gdn_gates(ba, gate_params, chunk, *, chunks_per_step=8):
    m = ba.shape[0]
    nc = m // chunk
    n = min(chunks_per_step, nc)
    assert nc * chunk == m and nc % n == 0
    return pl.pallas_call(
        functools.partial(_gdn_gates_kernel, chunk=chunk),
        grid=(nc // n,),
        in_specs=[pl.BlockSpec((n * chunk, LANES), lambda i: (i, 0)),
                  pl.BlockSpec((2, LANES), lambda i: (0, 0))],
        out_specs=[pl.BlockSpec((n * chunk, LANES), lambda i: (i, 0)),
                   pl.BlockSpec((n, LANES, chunk), lambda i: (i, 0, 0))],
        out_shape=[jax.ShapeDtypeStruct((m, LANES), F32), jax.ShapeDtypeStruct((nc, LANES, chunk), F32)],
        compiler_params=_cparams("parallel"),
        name="gdn_gates",
    )(ba, gate_params)


def _gdn_rule_kernel(q_ref, k_ref, v_ref, t_ref, r_ref, z_ref, wn_ref, s0_ref, o_ref, s_ref):
    C = q_ref.shape[0]
    nh = GDN_V_HEADS
    row = lax.broadcasted_iota(jnp.int32, (C, C), 0)
    col = lax.broadcasted_iota(jnp.int32, (C, C), 1)
    incl = (row >= col)[None]
    off_diag = (row != col)[None]
    eye = jnp.where(row == col, 1.0, 0.0).astype(F32)[None]
    n_levels = int(math.log2(C))
    assert 1 << n_levels == C
    bmm = functools.partial(jnp.einsum, "gij,gjk->gik", preferred_element_type=F32)

    def split(a):
        hi = a.astype(BF16)
        return hi, (a - hi.astype(F32)).astype(BF16)

    @pl.when(pl.program_id(1) == 0)
    def _():
        s_ref[...] = s0_ref[0]

    t_cols = t_ref[...]
    t_rows = r_ref[0]
    b_col = jnp.stack([t_cols[:, r:r + 1] for r in range(nh)])
    gc_col = jnp.stack([t_cols[:, nh + r:nh + r + 1] for r in range(nh)])
    b_row = jnp.stack([t_rows[r:r + 1, :] for r in range(nh)])
    gc_row = jnp.stack([t_rows[nh + r:nh + r + 1, :] for r in range(nh)])
    qk_heads = [(r * GDN_QK_HEADS) // GDN_V_HEADS for r in range(nh)]
    q = jnp.stack([q_ref[:, h * GDN_DK:(h + 1) * GDN_DK] for h in qk_heads])
    k = jnp.stack([k_ref[:, h * GDN_DK:(h + 1) * GDN_DK] for h in qk_heads])
    v = jnp.stack([v_ref[:, r * GDN_DV:(r + 1) * GDN_DV] for r in range(nh)])
    g_last = gc_row[:, :, C - 1:C]
    decay = jnp.exp(jnp.where(incl, gc_col - gc_row, -jnp.inf))
    kb = k.astype(BF16)
    qkk = jnp.einsum("gik,gjk->gij", jnp.concatenate([q.astype(BF16), kb], axis=1), kb,
                     preferred_element_type=F32)
    p0 = -jnp.where(off_diag, qkk[:, C:] * decay * b_col, 0.0)
    pb = p0.astype(BF16)
    x = eye + p0
    p = bmm(pb, pb)
    for lvl in range(1, n_levels):
        pb = p.astype(BF16)
        if lvl + 1 < n_levels:
            xp = bmm(pb, jnp.concatenate([x, p], axis=2).astype(BF16))
            x = x + xp[:, :, :C]
            p = xp[:, :, C:]
        else:
            x = x + bmm(pb, x.astype(BF16))
    t0 = x.astype(BF16)
    ah, al = split(eye - p0)
    at = bmm(jnp.concatenate([ah, al], axis=1), t0)
    resid = eye - (at[:, :C] + at[:, C:])
    t = t0.astype(F32) + bmm(t0, resid.astype(BF16))
    tb = t * b_row
    tbh, tbl = split(tb)
    vh, vl = split(v)
    uu = bmm(tbh, jnp.concatenate([vh, vl], axis=2))
    u = uu[:, :, :GDN_DV] + uu[:, :, GDN_DV:] + bmm(tbl, vh)
    w = bmm((tb * jnp.exp(gc_row)).astype(BF16), kb).astype(BF16)
    qkd = (qkk[:, :C] * decay).astype(BF16)
    qe = (q * jnp.exp(gc_col)).astype(BF16)
    kd = (k * jnp.exp(g_last - gc_col)).astype(BF16)

    s = s_ref[0]
    r_ = jnp.einsum("hck,hkv->hcv", jnp.concatenate([w, qe], axis=1), s.astype(BF16), preferred_element_type=F32)
    vb = (u - r_[:, :C]).astype(BF16)
    o = r_[:, C:] + jnp.einsum("hij,hjv->hiv", qkd, vb, preferred_element_type=F32)
    s_ref[0] = s * jnp.exp(g_last) + jnp.einsum("hck,hcv->hkv", kd, vb, preferred_element_type=F32)
    o = o * lax.rsqrt(jnp.mean(o * o, axis=-1, keepdims=True) + EPS) * wn_ref[...]
    for h in range(nh):
        z = z_ref[:, h * GDN_DV:(h + 1) * GDN_DV]
        o_ref[:, h * GDN_DV:(h + 1) * GDN_DV] = (o[h] * (z * _sigmoid(z))).astype(BF16)


def gdn_chunk_rule(conv, gates_t, gates_r, qkvz, w_norm, s0, layer, seq_len, chunk):
    m = conv.shape[0]
    b = m // seq_len
    nc = seq_len // chunk
    assert b * seq_len == m and nc * chunk == seq_len
    nqk = GDN_QK_HEADS * GDN_DK
    rows = lambda bi, c: bi * nc + c
    return pl.pallas_call(
        _gdn_rule_kernel,
        grid=(b, nc),
        in_specs=[pl.BlockSpec((chunk, nqk), lambda bi, c: (rows(bi, c), 0)),
                  pl.BlockSpec((chunk, nqk), lambda bi, c: (rows(bi, c), 1)),
                  pl.BlockSpec((chunk, GDN_Z), lambda bi, c: (rows(bi, c), 2 * nqk // GDN_Z)),
                  pl.BlockSpec((chunk, LANES), lambda bi, c: (rows(bi, c), 0)),
                  pl.BlockSpec((1, LANES, chunk), lambda bi, c: (rows(bi, c), 0, 0)),
                  pl.BlockSpec((chunk, GDN_Z), lambda bi, c: (rows(bi, c), GDN_QKV // GDN_Z)),
                  pl.BlockSpec((1, GDN_DV), lambda bi, c: (0, 0)),
                  pl.BlockSpec((1, 1, GDN_V_HEADS, GDN_DK, GDN_DV), lambda bi, c: (layer, bi, 0, 0, 0))],
        out_specs=[pl.BlockSpec((chunk, GDN_Z), lambda bi, c: (rows(bi, c), 0)),
                   pl.BlockSpec((1, GDN_V_HEADS, GDN_DK, GDN_DV), lambda bi, c: (bi, 0, 0, 0))],
        out_shape=[jax.ShapeDtypeStruct((m, GDN_Z), BF16),
                   jax.ShapeDtypeStruct((b, GDN_V_HEADS, GDN_DK, GDN_DV), F32)],
        compiler_params=_cparams("parallel", "arbitrary"),
        name="gdn_chunk_rule",
    )(conv, conv, conv, gates_t, gates_r, qkvz, w_norm.reshape(1, GDN_DV), s0)


def _toeplitz_bias(trow, rows):
    t = jnp.broadcast_to(trow, (rows, trow.shape[1]))
    return pltpu.roll(t, 0, axis=1, stride=1, stride_axis=0)


def _bias_rows(table):
    assert BAND_PAST == 2 * MAX_REL
    far = table[:, 2 * MAX_REL:]
    left = jnp.broadcast_to(far, (table.shape[0], MAX_REL))
    right = jnp.broadcast_to(far, (table.shape[0], 2 * BAND_PAST - 3 * MAX_REL - 1))
    return jnp.concatenate([left, table[:, ::-1], right], axis=1)[:, None, :]


def _att_prompt_kernel(q_ref, kp_ref, kc_ref, vp_ref, vc_ref, trow_ref, o_ref, bias_ref):
    i = pl.program_id(1)
    qb_rows = q_ref.shape[0]
    n_heads, pq, pk = bias_ref.shape
    n_parts = qb_rows // pq

    @pl.when(i == 0)
    def _():
        r = lax.broadcasted_iota(jnp.int32, (pq, pk), 0)
        w = lax.broadcasted_iota(jnp.int32, (pq, pk), 1)
        start = r - r % CHUNK
        in_band = jnp.logical_and(w >= start, w < start + BAND)
        for hd in range(n_heads):
            bias_ref[hd] = jnp.where(in_band, _toeplitz_bias(trow_ref[hd], pq)[:, :pk] * LOG2E, NEG_BIG)

    def body(first_block):
        col = lax.broadcasted_iota(jnp.int32, (pq, pk), 1)
        units = [(hd, p) for hd in range(n_heads) for p in range(n_parts)]
        ks, vs, ss, es, ls = [], [], [], [], []
        for hd in range(n_heads):
            hs = slice(hd * ATT_DH, (hd + 1) * ATT_DH)
            ks.append(jnp.concatenate([kp_ref[:, hs], kc_ref[:, hs]], axis=0).astype(BF16))
            vs.append(jnp.concatenate([vp_ref[:, hs], vc_ref[:, hs]], axis=0).astype(BF16))
        for hd, p in units:
            q = (q_ref[p * pq:(p + 1) * pq, hd * ATT_DH:(hd + 1) * ATT_DH] * (ATT_DH ** -0.5 * LOG2E)).astype(BF16)
            ss.append(_dot_nt(q, ks[hd][p * pq:p * pq + pk]) + bias_ref[hd])
        for (hd, p), s in zip(units, ss):
            if first_block:
                s = jnp.where(col < qb_rows - p * pq, NEG_BIG, s)
            e = jnp.exp2(s - jnp.max(s, axis=-1, keepdims=True))
            es.append(e.astype(BF16))
            ls.append(jnp.sum(e, axis=-1, keepdims=True))
        for (hd, p), e, l in zip(units, es, ls):
            o_ref[p * pq:(p + 1) * pq, hd * ATT_DH:(hd + 1) * ATT_DH] = (
                _dot(e, vs[hd][p * pq:p * pq + pk]) / l).astype(BF16)

    @pl.when(i == 0)
    def _():
        body(True)

    @pl.when(i > 0)
    def _():
        body(False)


def att_prompt(qkv, trows, *, heads_per_step=4):
    l = qkv.shape[0]
    qb = BAND_PAST
    assert l % qb == 0 and ATT_HEADS % heads_per_step == 0
    g = ATT_HEADS // heads_per_step
    wd = heads_per_step * ATT_DH
    prev = lambda i: jnp.maximum(i - 1, 0)
    return pl.pallas_call(
        _att_prompt_kernel,
        grid=(g, l // qb),
        in_specs=[pl.BlockSpec((qb, wd), lambda hh, i: (i, hh)),
                  pl.BlockSpec((qb, wd), lambda hh, i: (prev(i), g + hh)),
                  pl.BlockSpec((qb, wd), lambda hh, i: (i, g + hh)),
                  pl.BlockSpec((qb, wd), lambda hh, i: (prev(i), 2 * g + hh)),
                  pl.BlockSpec((qb, wd), lambda hh, i: (i, 2 * g + hh)),
                  pl.BlockSpec((heads_per_step, 1, 2 * qb), lambda hh, i: (hh, 0, 0))],
        out_specs=pl.BlockSpec((qb, wd), lambda hh, i: (i, hh)),
        out_shape=jax.ShapeDtypeStruct((l, ATT_HEADS * ATT_DH), BF16),
        scratch_shapes=[pltpu.VMEM((heads_per_step, qb // 4, qb // 4 + BAND_PAST), F32)],
        compiler_params=_cparams("parallel", "arbitrary"),
        name="att_prompt",
    )(qkv, qkv, qkv, qkv, qkv, trows)


def _att_sample_kernel(qkv_ref, ck_ref, cv_ref, trow_ref, o_ref):
    d = ATT_HEADS * ATT_DH
    n_new = qkv_ref.shape[0]
    n_old = ck_ref.shape[2]
    ck_t = jnp.swapaxes(ck_ref[0, 0], 0, 1)
    cv_t = jnp.swapaxes(cv_ref[0, 0], 0, 1)
    heads = range(ATT_HEADS)
    cols = lambda h, section: slice(section * d + h * ATT_DH, section * d + (h + 1) * ATT_DH)
    s1, s2, e1, e2, ls = [], [], [], [], []
    for h in heads:
        q = (qkv_ref[:, cols(h, 0)] * (ATT_DH ** -0.5 * LOG2E)).astype(BF16)
        bias = _toeplitz_bias(trow_ref[h], n_new) * LOG2E
        s1.append(_dot_nt(q, ck_t[h].astype(BF16)) + bias[:, :n_old])
        s2.append(_dot_nt(q, qkv_ref[:, cols(h, 1)].astype(BF16)) + bias[:, n_old:n_old + n_new])
    for h in heads:
        m = jnp.maximum(jnp.max(s1[h], axis=-1, keepdims=True), jnp.max(s2[h], axis=-1, keepdims=True))
        a1 = jnp.exp2(s1[h] - m)
        a2 = jnp.exp2(s2[h] - m)
        ls.append(jnp.sum(a1, axis=-1, keepdims=True) + jnp.sum(a2, axis=-1, keepdims=True))
        e1.append(a1.astype(BF16))
        e2.append(a2.astype(BF16))
    for h in heads:
        o = _dot(e1[h], cv_t[h].astype(BF16)) + _dot(e2[h], qkv_ref[:, cols(h, 2)].astype(BF16))
        o_ref[:, cols(h, 0)] = (o / ls[h]).astype(BF16)


def att_sample(qkv, cache_k, cache_v, layer, trows, seq_len):
    m = qkv.shape[0]
    b = m // seq_len
    r = cache_k.shape[2]
    d = ATT_HEADS * ATT_DH
    assert r == BAND_PAST and r + seq_len <= 2 * BAND_PAST
    return pl.pallas_call(
        _att_sample_kernel,
        grid=(b,),
        in_specs=[pl.BlockSpec((seq_len, 3 * d), lambda i: (i, 0)),
                  pl.BlockSpec((1, 1, r, ATT_HEADS, ATT_DH), lambda i: (layer, i, 0, 0, 0)),
                  pl.BlockSpec((1, 1, r, ATT_HEADS, ATT_DH), lambda i: (layer, i, 0, 0, 0)),
                  pl.BlockSpec((ATT_HEADS, 1, 2 * BAND_PAST), lambda i: (0, 0, 0))],
        out_specs=pl.BlockSpec((seq_len, d), lambda i: (i, 0)),
        out_shape=jax.ShapeDtypeStruct((m, d), BF16),
        compiler_params=_cparams("parallel"),
        name="att_sample",
    )(qkv, cache_k, cache_v, trows)


def _gdn_mix(x, qkvz, ba, conv_state, rec0, rec_layer, w_conv, gate_params, w_norm, seq_len, chunk, tr):
    conv8 = jnp.pad(conv_state, ((0, 0), (SUBLANES - (CONV_W - 1), 0), (0, 0)))
    conv = gdn_prep(qkvz, conv8, w_conv, seq_len, tr=tr)
    gates_t, gates_r = gdn_gates(ba, gate_params, chunk)
    o, rec = gdn_chunk_rule(conv, gates_t, gates_r, qkvz, w_norm, rec0, rec_layer, seq_len, chunk)
    nb = x.shape[0] // seq_len
    new_conv = qkvz.reshape(nb, seq_len, -1)[:, seq_len - (CONV_W - 1):, :GDN_QKV]
    return o, new_conv, rec


def _gdn_layer(xp, xs, nw, j, w_in_all, w_conv, a_log, dt_bias, w_norm, w_out_all, state_rec_all, state_conv, dec_seq):
    gate_params = jnp.zeros((2, LANES), F32)
    gate_params = gate_params.at[0, GDN_V_HEADS:2 * GDN_V_HEADS].set(a_log)
    gate_params = gate_params.at[1, GDN_V_HEADS:2 * GDN_V_HEADS].set(dt_bias)
    w_in_t = jnp.swapaxes(w_in_all, 1, 2)
    qkvz_s, ba_s, w_main, w_ba = norm_proj_cast(xs, nw, w_in_t, j, GDN_QKV + GDN_Z, n2=2 * GDN_V_HEADS,
                                                transposed=True)
    o_s, conv_s, rec_s = _gdn_mix(xs, qkvz_s, ba_s, state_conv, state_rec_all, j, w_conv, gate_params, w_norm,
                                  dec_seq, dec_seq, dec_seq)
    xs_new, w_out_b = out_proj_cast(xs, o_s, w_out_all, j)

    lp = xp.shape[0]
    qkvz_p, ba_p = norm_proj(xp, nw, w_main, w2=w_ba, tm=1024)
    o_p, conv_p, rec_p = _gdn_mix(xp, qkvz_p, ba_p, jnp.zeros((1, CONV_W - 1, GDN_QKV), F32),
                                  jnp.zeros((1, 1, GDN_V_HEADS, GDN_DK, GDN_DV), F32), 0, w_conv, gate_params, w_norm,
                                  lp, CHUNK, 256)
    xp_new = out_proj(xp, o_p, w_out_b, tm=1024)
    return (xp_new, conv_p, rec_p), (xs_new, conv_s, rec_s)


def _att_layer(xp, xs, nw, j, w_qkv_all, b_qkv, table, w_o_all, b_o, cache_k, cache_v, dec_seq):
    d = xp.shape[1]
    trows = _bias_rows(table)
    nb = cache_k.shape[1]
    qkv_s, w_qkv_b = norm_proj_cast(xs, nw, w_qkv_all, j, 3 * d, bias=b_qkv)
    o_s = att_sample(qkv_s, cache_k, cache_v, j, trows, dec_seq)
    xs_new, w_o_b = out_proj_cast(xs, o_s, w_o_all, j, bias=b_o)
    ks = qkv_s[:, d:2 * d].reshape(nb, dec_seq, ATT_HEADS, ATT_DH)
    vs = qkv_s[:, 2 * d:].reshape(nb, dec_seq, ATT_HEADS, ATT_DH)
    qkv_p = norm_proj(xp, nw, w_qkv_b, bias=b_qkv, tm=1024)
    o_p = att_prompt(qkv_p, trows)
    xp_new = out_proj(xp, o_p, w_o_b, bias=b_o, tn=d)
    lp = xp.shape[0]
    keep = min(BAND_PAST, lp)
    kp = qkv_p[lp - keep:, d:2 * d].reshape(1, keep, ATT_HEADS, ATT_DH)
    vp = qkv_p[lp - keep:, 2 * d:].reshape(1, keep, ATT_HEADS, ATT_DH)
    return (xp_new, kp, vp), (xs_new, ks, vs)


def kernel(x_prompt, x_sample, state_gdn_rec, state_gdn_conv, cache_att_k, cache_att_v, norm_mix, norm_ffn, norm_final, gdn_w_in, gdn_w_conv, gdn_a_log, gdn_dt_bias, gdn_w_norm, gdn_w_out, att_w_qkv, att_b_qkv, att_rel_bias, att_w_o, att_b_o, ffn_w_gate, ffn_w_up, ffn_w_down):
    bp, lp, d = x_prompt.shape
    bs, ls, _ = x_sample.shape
    assert bp == 1
    depth = norm_mix.shape[0]
    xp = x_prompt.reshape(bp * lp, d)
    xs = x_sample.reshape(bs * ls, d)
    p_rec, p_conv, p_k, p_v = [], [], [], []
    s_rec, s_conv, s_k, s_v = [], [], [], []
    for layer in range(depth):
        j = layer // 2
        if layer % 2 == 0:
            (xp, cp, rp), (xs, cs, rs) = _gdn_layer(
                xp, xs, norm_mix[layer], j, gdn_w_in, gdn_w_conv[j], gdn_a_log[j], gdn_dt_bias[j], gdn_w_norm[j],
                gdn_w_out, state_gdn_rec, state_gdn_conv[j], ls)
            p_conv.append(cp)
            p_rec.append(rp)
            s_conv.append(cs)
            s_rec.append(rs)
        else:
            (xp, kp, vp), (xs, kn, vn) = _att_layer(
                xp, xs, norm_mix[layer], j, att_w_qkv, att_b_qkv[j], att_rel_bias[j], att_w_o, att_b_o[j],
                cache_att_k, cache_att_v, ls)
            p_k.append(kp)
            p_v.append(vp)
            s_k.append(kn)
            s_v.append(vn)
        final_nw = norm_final if layer == depth - 1 else None
        xs, wg, wu, wd = ffn_cast(xs, norm_ffn[layer], ffn_w_gate, ffn_w_up, ffn_w_down, layer, final_nw)
        xp = ffn(xp, norm_ffn[layer], wg, wu, wd, final_nw)
    y_prompt = xp.reshape(bp, lp, d)
    y_sample = xs.reshape(bs, ls, d)
    return (y_prompt, y_sample, jnp.stack(p_rec), jnp.stack(p_conv), jnp.stack(p_k), jnp.stack(p_v),
            jnp.stack(s_rec), jnp.stack(s_conv), jnp.stack(s_k), jnp.stack(s_v))
```
